```python
import jax, jax.numpy as jnp
from jax import lax
import numpy as np

D_MODEL = 2048
BATCH = 2
SEQ = 16384
DEPTH = 2
DEC_BATCH = 32
DEC_SEQ = 16
PAST_LEN = 4096

CHUNK = 64
N_BRANCH = 4
BRANCH_WIDTH = D_MODEL // N_BRANCH
HEAD_DIM = 128
N_HEADS = BRANCH_WIDTH // HEAD_DIM
CONV_WIDTH = 31
CONV_CH = BRANCH_WIDTH
Q_BLOCK = 128
ROPE_BASE = 10000.0
NORM_EPS = 1e-6
LN_EPS = 1e-5
N_IN = 16 * BRANCH_WIDTH + 2 * N_HEADS + N_BRANCH * D_MODEL

kernel_name = 'streaming_hybrid_gated_encoder'


def _split_cols(p):
    W, H, D = BRANCH_WIDTH, N_HEADS, D_MODEL
    sizes = [W] * 4 + [W] * 5 + [H, H] + [W] * 4 + [W] * 3 + [D] * N_BRANCH
    idx = np.cumsum(sizes)[:-1].tolist()
    return jnp.split(p, idx, axis=-1)


def _rmsnorm(x, g):
    x32 = x.astype(jnp.float32)
    y = x32 * lax.rsqrt(jnp.mean(x32 * x32, axis=-1, keepdims=True) + NORM_EPS)
    return (y * g.astype(jnp.float32)).astype(x.dtype)


def _head_norm(x):
    mu = jnp.mean(x, axis=-1, keepdims=True)
    xc = x - mu
    return xc * lax.rsqrt(jnp.mean(xc * xc, axis=-1, keepdims=True) + LN_EPS)


def _layernorm(x, g, b):
    x32 = x.astype(jnp.float32)
    return _head_norm(x32) * g.astype(jnp.float32) + b.astype(jnp.float32)


def _rotary(x, pos):
    half = HEAD_DIM // 2
    inv = ROPE_BASE ** (-jnp.arange(half, dtype=jnp.float32) / half)
    ang = pos.astype(jnp.float32)[:, None] * inv[None, :]
    cos = jnp.cos(ang)[None, :, None, :]
    sin = jnp.sin(ang)[None, :, None, :]
    x1, x2 = x[..., :half], x[..., half:]
    return jnp.concatenate([x1 * cos - x2 * sin, x1 * sin + x2 * cos], axis=-1)


def _sb_block(q, k, v, q_pos, k_pos):
    z = jnp.einsum('bqhd,bkhd->bhqk', q, k).astype(jnp.float32) * (HEAD_DIM ** -0.5)
    causal = (k_pos[None, :] < q_pos[:, None])[None, None]
    log_1mb = jnp.where(causal, jax.nn.log_sigmoid(-z), 0.0)
    later = lax.cumsum(log_1mb, axis=3, reverse=True) - log_1mb
    w = jnp.where(causal, jnp.exp(jax.nn.log_sigmoid(z) + later), 0.0)
    return jnp.einsum('bhqk,bkhd->bqhd', w.astype(v.dtype), v)


def _stick_breaking(q, k, v, q_pos, k_pos):
    B, T, H, d = q.shape
    blk = min(Q_BLOCK, T)
    nb = T // blk
    qb = q.reshape(B, nb, blk, H, d).transpose(1, 0, 2, 3, 4)
    pb = q_pos.reshape(nb, blk)
    out = lax.map(lambda a: _sb_block(a[0], k, v, a[1], k_pos), (qb, pb))
    return out.transpose(1, 0, 2, 3, 4).reshape(B, T, H, d)


def _mlstm(q, k, v, i_pre, log_f, C0, n0, m0):
    B, T, H, d = q.shape
    L = min(CHUNK, T)
    nc = T // L
    to_chunks = lambda a: a.reshape((B, nc, L) + a.shape[2:]).swapaxes(0, 1)
    tri = jnp.tril(jnp.ones((L, L), bool))

    def step(carry, xs):
        C, n, m = carry
        qc, kc, vc, ic, fc = xs
        b = jnp.cumsum(fc, axis=1).swapaxes(1, 2)
        iT = ic.swapaxes(1, 2)
        logD = jnp.where(tri, b[..., :, None] - b[..., None, :] + iT[..., None, :], -jnp.inf)
        inter = m[..., None] + b
        m_row = jnp.maximum(inter, jnp.max(logD, axis=-1))
        w = jnp.exp(logD - m_row[..., None]) * jnp.einsum('bthd,bshd->bhts', qc, kc)
        a_inter = jnp.exp(inter - m_row)
        num = (jnp.einsum('bhts,bshe->bthe', w, vc)
               + a_inter.swapaxes(1, 2)[..., None] * jnp.einsum('bthd,bhde->bthe', qc, C))
        den = jnp.sum(w, axis=-1) + a_inter * jnp.einsum('bthd,bhd->bht', qc, n)
        h = num / jnp.maximum(jnp.abs(den), jnp.exp(-m_row)).swapaxes(1, 2)[..., None]
        g = b[..., -1:] - b + iT
        m_new = jnp.maximum(m + b[..., -1], jnp.max(g, axis=-1))
        decay = jnp.exp(m + b[..., -1] - m_new)
        wg = jnp.exp(g - m_new[..., None])
        C_new = decay[..., None, None] * C + jnp.einsum('bhs,bshd,bshe->bhde', wg, kc, vc)
        n_new = decay[..., None] * n + jnp.einsum('bhs,bshd->bhd', wg, kc)
        return (C_new, n_new, m_new), h

    (C, n, m), hs = lax.scan(step, (C0, n0, m0), tuple(to_chunks(a) for a in (q, k, v, i_pre, log_f)))
    return hs.swapaxes(0, 1).reshape(B, T, H, d), C, n, m


def _retention(q, k, v, S0):
    B, T, H, d = q.shape
    L = min(CHUNK, T)
    nc = T // L
    lg = jnp.log1p(-jnp.exp2(-5.0 - jnp.arange(H, dtype=jnp.float32)))
    t = jnp.arange(L, dtype=jnp.float32)
    rel = t[:, None] - t[None, :]
    decay_mask = jnp.where(rel >= 0, jnp.exp(lg[:, None, None] * jnp.maximum(rel, 0.0)), 0.0)
    q_decay = jnp.exp(lg[:, None] * (t + 1.0)).T[None, :, :, None]
    k_decay = jnp.exp(lg[:, None] * (L - 1.0 - t))
    chunk_decay = jnp.exp(lg * L)[:, None, None]
    to_chunks = lambda a: a.reshape(B, nc, L, H, d).swapaxes(0, 1)

    def step(S, xs):
        qc, kc, vc = xs
        att = jnp.einsum('bthd,bshd->bhts', qc, kc) * decay_mask
        o = jnp.einsum('bhts,bshe->bthe', att, vc) + jnp.einsum('bthd,bhde->bthe', qc, S) * q_decay
        S_new = chunk_decay * S + jnp.einsum('bshd,bshe,hs->bhde', kc, vc, k_decay)
        return S_new, o

    S, os_ = lax.scan(step, S0, (to_chunks(q), to_chunks(k), to_chunks(v)))
    return os_.swapaxes(0, 1).reshape(B, T, H, d), S


def _causal_dwconv(u, buf, w, b):
    xp = jnp.concatenate([buf.astype(u.dtype), u], axis=1)
    y = lax.conv_general_dilated(xp, w[:, None, :].astype(u.dtype), window_strides=(1,), padding='VALID',
                                 dimension_numbers=('NWC', 'WIO', 'NWC'), feature_group_count=u.shape[-1])
    return y + b.astype(u.dtype), xp[:, xp.shape[1] - (CONV_WIDTH - 1):]


def _layer(x, pos, past_kv, C0, n0, m0, S0, conv0,
           norm_g, w_in, b_i, b_f, conv_w, conv_b, ln_g, ln_b, w_branch, w_out):
    B, T, _ = x.shape
    f32 = jnp.float32
    dt = x.dtype
    hin = _rmsnorm(x, norm_g)
    (qa, ka, va, za, qb, kb, vb, ob, zb, ib, fb,
     qc, kc, vc, zc, ud, gd, zd, *gate_pre) = _split_cols(hin @ w_in)
    heads = lambda a: a.reshape(B, T, N_HEADS, HEAD_DIM)

    ka_h, va_h = heads(ka), heads(va)
    if past_kv is None:
        k_all, v_all = ka_h, va_h
    else:
        k_all = jnp.concatenate([past_kv[0].astype(dt), ka_h], axis=1)
        v_all = jnp.concatenate([past_kv[1].astype(dt), va_h], axis=1)
    k_pos = jnp.arange(k_all.shape[1])
    ya = _stick_breaking(heads(qa), k_all, v_all, pos, k_pos).reshape(B, T, BRANCH_WIDTH) * jax.nn.silu(za)

    hb, C1, n1, m1 = _mlstm(heads(qb).astype(f32), heads(kb).astype(f32) * (HEAD_DIM ** -0.5),
                            heads(vb).astype(f32), ib.astype(f32) + b_i.astype(f32),
                            jax.nn.log_sigmoid(fb.astype(f32) + b_f.astype(f32)),
                            C0.astype(f32), n0.astype(f32), m0.astype(f32))
    yb = (_head_norm(hb).reshape(B, T, BRANCH_WIDTH) * jax.nn.sigmoid(ob.astype(f32))).astype(dt) * jax.nn.silu(zb)

    qr = _rotary(heads(qc).astype(f32), pos)
    kr = _rotary(heads(kc).astype(f32), pos) * (HEAD_DIM ** -0.5)
    hc, S1 = _retention(qr, kr, heads(vc).astype(f32), S0.astype(f32))
    yc = _head_norm(hc).reshape(B, T, BRANCH_WIDTH).astype(dt) * jax.nn.silu(zc)

    glu = ud * jax.nn.sigmoid(gd)
    cv, conv1 = _causal_dwconv(glu, conv0, conv_w, conv_b)
    yd = jax.nn.silu(_layernorm(cv, ln_g, ln_b)).astype(dt) * jax.nn.silu(zd)

    merged = jax.nn.sigmoid(gate_pre[0]) * (ya @ w_branch[0])
    merged = merged + jax.nn.sigmoid(gate_pre[1]) * (yb @ w_branch[1])
    merged = merged + jax.nn.sigmoid(gate_pre[2]) * (yc @ w_branch[2])
    merged = merged + jax.nn.sigmoid(gate_pre[3]) * (yd @ w_branch[3])
    new_state = (ka_h, va_h, C1.astype(dt), n1.astype(dt), m1.astype(dt), S1.astype(dt), conv1)
    return x + merged @ w_out, new_state


def _trunk(x, past_len, cache_k, cache_v, C0, n0, m0, S0, conv0,
           norm_g, w_in, b_i, b_f, conv_w, conv_b, ln_g, ln_b, w_branch, w_out, final_g):
    T = x.shape[1]
    pos = past_len + jnp.arange(T)
    per_layer = []
    for l in range(DEPTH):
        past = None if cache_k is None else (cache_k[l], cache_v[l])
        x, st = _layer(x, pos, past, C0[l], n0[l], m0[l], S0[l], conv0[l],
                       norm_g[l], w_in[l], b_i[l], b_f[l], conv_w[l], conv_b[l],
                       ln_g[l], ln_b[l], w_branch[l], w_out[l])
        per_layer.append(st)
    stacked = tuple(jnp.stack(s, axis=0) for s in zip(*per_layer))
    return _rmsnorm(x, final_g), stacked


def setup_inputs(seed: int = 0) -> dict:
    key = jax.random.key(seed)
    ks = jax.random.split(key, 20)
    H, HD, W, D = N_HEADS, HEAD_DIM, BRANCH_WIDTH, D_MODEL
    nrm = lambda k, shape, s=1.0: s * jax.random.normal(k, shape, jnp.float32)
    return {
        'x_prompt': nrm(ks[0], (BATCH, SEQ, D)),
        'x_sample': nrm(ks[1], (DEC_BATCH, DEC_SEQ, D)),
        'cache_sb_k': nrm(ks[2], (DEPTH, DEC_BATCH, PAST_LEN, H, HD)),
        'cache_sb_v': nrm(ks[3], (DEPTH, DEC_BATCH, PAST_LEN, H, HD)),
        'state_mlstm_C': nrm(ks[4], (DEPTH, DEC_BATCH, H, HD, HD), 0.5),
        'state_mlstm_n': nrm(ks[5], (DEPTH, DEC_BATCH, H, HD), 0.5),
        'state_mlstm_m': nrm(ks[6], (DEPTH, DEC_BATCH, H)),
        'state_ret_S': nrm(ks[7], (DEPTH, DEC_BATCH, H, HD, HD)),
        'state_conv': nrm(ks[8], (DEPTH, DEC_BATCH, CONV_WIDTH - 1, CONV_CH), 0.5),
        'norm_g': 1.0 + nrm(ks[9], (DEPTH, D), 0.02),
        'w_in': nrm(ks[10], (DEPTH, D, N_IN), D ** -0.5),
        'mlstm_b_i': nrm(ks[11], (DEPTH, H), 0.1),
        'mlstm_b_f': jnp.linspace(3.0, 6.0, H, dtype=jnp.float32)[None, :] + nrm(ks[12], (DEPTH, H), 0.1),
        'conv_w': nrm(ks[13], (DEPTH, CONV_WIDTH, CONV_CH), CONV_WIDTH ** -0.5),
        'conv_b': nrm(ks[14], (DEPTH, CONV_CH), 0.02),
        'conv_ln_g': 1.0 + nrm(ks[15], (DEPTH, CONV_CH), 0.02),
        'conv_ln_b': nrm(ks[16], (DEPTH, CONV_CH), 0.02),
        'w_branch': nrm(ks[17], (DEPTH, N_BRANCH, W, D), W ** -0.5),
        'w_out': nrm(ks[18], (DEPTH, D, D), D ** -0.5),
        'final_g': 1.0 + nrm(ks[19], (D,), 0.02),
    }


def reference(x_prompt, x_sample, cache_sb_k, cache_sb_v, state_mlstm_C, state_mlstm_n, state_mlstm_m,
              state_ret_S, state_conv, norm_g, w_in, mlstm_b_i, mlstm_b_f, conv_w, conv_b,
              conv_ln_g, conv_ln_b, w_branch, w_out, final_g):
    Bp = x_prompt.shape[0]
    zC = jnp.zeros((DEPTH, Bp, N_HEADS, HEAD_DIM, HEAD_DIM), jnp.float32)
    zn = jnp.zeros((DEPTH, Bp, N_HEADS, HEAD_DIM), jnp.float32)
    zm = jnp.zeros((DEPTH, Bp, N_HEADS), jnp.float32)
    zconv = jnp.zeros((DEPTH, Bp, CONV_WIDTH - 1, CONV_CH), x_prompt.dtype)
    y_prompt, (pk, pv, pC, pn, pm, pS, pconv) = _trunk(
        x_prompt, 0, None, None, zC, zn, zm, zC, zconv,
        norm_g, w_in, mlstm_b_i, mlstm_b_f, conv_w, conv_b, conv_ln_g, conv_ln_b, w_branch, w_out, final_g)
    y_sample, (sk, sv, sC, sn, sm, sS, sconv) = _trunk(
        x_sample, cache_sb_k.shape[2], cache_sb_k, cache_sb_v, state_mlstm_C, state_mlstm_n, state_mlstm_m,
        state_ret_S, state_conv,
        norm_g, w_in, mlstm_b_i, mlstm_b_f, conv_w, conv_b, conv_ln_g, conv_ln_b, w_branch, w_out, final_g)
    return (y_prompt, y_sample, pk, pv, pC, pn, pm, pS, pconv, sk, sv, sC, sn, sm, sS, sconv)
```

```python
import functools

import numpy as np
import jax
import jax.numpy as jnp
from jax import lax
from jax.experimental import pallas as pl
from jax.experimental.pallas import tpu as pltpu

F32 = jnp.float32
BF16 = jnp.bfloat16

HEAD_DIM = 128
N_BRANCH = 4
CONV_WIDTH = 31
ROPE_BASE = 10000.0
NORM_EPS = 1e-6
LN_EPS = 1e-5
LANES = 128
HALO = 32
NEG = -1e30
SB_CUT = 104.0
VMEM_LIMIT = 56 * 1024 * 1024

QA, KA, VA, ZA, QB, KB, VB, OB, ZB, QC, KC, VC, ZC, UD, GD, ZD = range(16)


def _cparams(n_axes):
    return pltpu.CompilerParams(dimension_semantics=("arbitrary",) * n_axes,
                                vmem_limit_bytes=VMEM_LIMIT)


def _dot(a, b):
    return jnp.dot(a, b, preferred_element_type=F32)


def _dot_nt(a, b):
    return lax.dot_general(a, b, (((1,), (1,)), ((), ())), preferred_element_type=F32)


def _softplus(x):
    return jnp.maximum(x, 0.0) + jnp.log1p(jnp.exp(-jnp.abs(x)))


def _log_sigmoid(x):
    return -_softplus(-x)


def _sigmoid(x):
    return 1.0 / (1.0 + jnp.exp(-x))


def _silu(x):
    return x * _sigmoid(x)


def _head_norm(x):
    mu = jnp.mean(x, axis=-1, keepdims=True)
    xc = x - mu
    return xc * lax.rsqrt(jnp.mean(xc * xc, axis=-1, keepdims=True) + LN_EPS)


def _split_dot(x, m, terms):
    out = None
    r = x
    for t in range(terms):
        p = r.astype(BF16)
        d = _dot(p, m)
        out = d if out is None else out + d
        if t + 1 < terms:
            r = r - p.astype(F32)
    return out


def _pad_rows(x, rows):
    if x.shape[0] == rows:
        return x
    return jnp.concatenate([x, jnp.zeros((rows - x.shape[0],) + x.shape[1:], x.dtype)], axis=0)


def _inproj_kernel(x_ref, g_ref, w_ref, wif_ref, pg_ref, k_ref, v_ref, if_ref, hin_ref, *, W, rows):
    j = pl.program_id(1)
    tm = x_ref.shape[0]

    @pl.when(j == 0)
    def _():
        for r in range(0, tm, rows):
            x = x_ref[r:r + rows, :]
            ms = jnp.mean(x * x, axis=-1, keepdims=True)
            hb = ((x * lax.rsqrt(ms + NORM_EPS)) * g_ref[...]).astype(BF16)
            hin_ref[r:r + rows, :] = hb
            if_ref[r:r + rows, :] = _dot(hb, wif_ref[...])

    for r in range(0, tm, rows):
        acc = _dot(hin_ref[r:r + rows, :], w_ref[...])
        pg_ref[r:r + rows, :] = acc.astype(BF16)

        @pl.when(j == 0)
        def _():
            k_ref[r:r + rows, :] = acc[:, KA * W:(KA + 1) * W]

        @pl.when(j == 1)
        def _():
            v_ref[r:r + rows, :] = acc[:, (VA - 2) * W:(VA - 1) * W]


def _inproj(x2, norm_g, w_main, w_if, tm):
    N, D = x2.shape
    NC = w_main.shape[1]
    W = D // N_BRANCH
    tn = 2 * W
    rows = min(tm, 256)
    kern = functools.partial(_inproj_kernel, W=W, rows=rows)
    return pl.pallas_call(
        kern,
        grid=(N // tm, NC // tn),
        in_specs=[
            pl.BlockSpec((tm, D), lambda i, j: (i, 0)),
            pl.BlockSpec((1, D), lambda i, j: (0, 0)),
            pl.BlockSpec((D, tn), lambda i, j: (0, j)),
            pl.BlockSpec((D, LANES), lambda i, j: (0, 0)),
        ],
        out_specs=[
            pl.BlockSpec((tm, tn), lambda i, j: (i, j)),
            pl.BlockSpec((tm, W), lambda i, j: (i, 0)),
            pl.BlockSpec((tm, W), lambda i, j: (i, 0)),
            pl.BlockSpec((tm, LANES), lambda i, j: (i, 0)),
        ],
        out_shape=[
            jax.ShapeDtypeStruct((N, NC), BF16),
            jax.ShapeDtypeStruct((N, W), F32),
            jax.ShapeDtypeStruct((N, W), F32),
            jax.ShapeDtypeStruct((N, LANES), F32),
        ],
        scratch_shapes=[pltpu.VMEM((tm, D), BF16)],
        compiler_params=_cparams(2),
        name="inproj",
    )(x2, norm_g.reshape(1, D), w_main, w_if)


def _sb_kernel(q_ref, kn_ref, vn_ref, z_ref, kp_ref, vp_ref, o_ref, acc_ref, car_ref, *, tq, tk, past0):
    qi = pl.program_id(2)
    scale = HEAD_DIM ** -0.5
    q = q_ref[...]
    tkd = max(tq, LANES)
    kn = _pad_rows(kn_ref[...], tkd)
    vn = _pad_rows(vn_ref[...], tkd)

    def strict_upper(n):
        return (lax.broadcasted_iota(jnp.int32, (n, n), 0)
                > lax.broadcasted_iota(jnp.int32, (n, n), 1)).astype(BF16)

    z = _dot_nt(q, kn) * scale
    causal = (lax.broadcasted_iota(jnp.int32, (tq, tkd), 1)
              < lax.broadcasted_iota(jnp.int32, (tq, tkd), 0))
    sp = jnp.where(causal, _softplus(z), 0.0)
    cum = _split_dot(sp, strict_upper(tkd), 2)
    w = jnp.where(causal, jnp.exp(z - sp - cum), 0.0)
    acc_ref[...] = _dot(w.astype(BF16), vn)
    car_ref[...] = jnp.sum(sp, axis=1, keepdims=True)

    u_past = strict_upper(tk)
    n_blocks = (past0 + qi * tq) // tk

    def cond(c):
        kb, go = c
        return jnp.logical_and(kb >= 0, go)

    def body(c):
        kb, _ = c
        start = pl.multiple_of(kb * tk, tk)
        k = kp_ref[pl.ds(start, tk), :].astype(BF16)
        v = vp_ref[pl.ds(start, tk), :].astype(BF16)
        zb = _dot_nt(q, k) * scale
        spb = _softplus(zb)
        cumb = _split_dot(spb, u_past, 2)
        car = car_ref[...]
        wb = jnp.exp(zb - spb - cumb - car)
        acc_ref[...] += _dot(wb.astype(BF16), v)
        car_new = car + jnp.sum(spb, axis=1, keepdims=True)
        car_ref[...] = car_new
        return kb - 1, jnp.min(car_new) < SB_CUT

    lax.while_loop(cond, body, (n_blocks - 1, True))

    za = z_ref[...].astype(F32)
    o_ref[...] = (acc_ref[...] * _silu(za)).astype(BF16)


def _stick_breaking(pg3, past_k, past_v, H, tq, tk):
    B, T, NC = pg3.shape
    W = H * HEAD_DIM
    col = lambda g: (lambda b, h, i: (b, i, g * H + h))
    if past_k is None:
        kp, vp, past0 = pg3, pg3, 0
        kp_spec = pl.BlockSpec((None, T, HEAD_DIM), lambda b, h, i: (b, 0, KA * H + h))
        vp_spec = pl.BlockSpec((None, T, HEAD_DIM), lambda b, h, i: (b, 0, VA * H + h))
    else:
        kp, vp, past0 = past_k, past_v, past_k.shape[1]
        kp_spec = pl.BlockSpec((None, past0, HEAD_DIM), lambda b, h, i: (b, 0, h))
        vp_spec = pl.BlockSpec((None, past0, HEAD_DIM), lambda b, h, i: (b, 0, h))
    blk = lambda g: pl.BlockSpec((None, tq, HEAD_DIM), col(g))
    kern = functools.partial(_sb_kernel, tq=tq, tk=tk, past0=past0)
    return pl.pallas_call(
        kern,
        grid=(B, H, T // tq),
        in_specs=[blk(QA), blk(KA), blk(VA), blk(ZA), kp_spec, vp_spec],
        out_specs=pl.BlockSpec((None, tq, HEAD_DIM), lambda b, h, i: (b, i, h)),
        out_shape=jax.ShapeDtypeStruct((B, T, W), BF16),
        scratch_shapes=[pltpu.VMEM((tq, HEAD_DIM), F32), pltpu.VMEM((tq, 1), F32)],
        compiler_params=_cparams(3),
        name="stick_breaking",
    )(pg3, pg3, pg3, pg3, kp, vp)


def _mlstm_kernel(q_ref, k_ref, v_ref, o_ref, z_ref, if_ref, bias_ref, c0_ref, n0_ref, m0_ref,
                  y_ref, c_ref, n_ref, m_ref, *, L, Lc, H):
    ci = pl.program_id(1)
    scale = HEAD_DIM ** -0.5

    @pl.when(ci == 0)
    def _():
        c_ref[...] = c0_ref[...]
        n_ref[...] = n0_ref[...]
        m_ref[...] = m0_ref[...]

    ifv = _pad_rows(if_ref[...] + bias_ref[...], Lc)
    valid = lax.broadcasted_iota(jnp.int32, (Lc, LANES), 0) < L
    ipre = jnp.where(valid, ifv, NEG)
    logf = jnp.where(valid, _log_sigmoid(ifv), 0.0)
    tri = (lax.broadcasted_iota(jnp.int32, (Lc, Lc), 1)
           <= lax.broadcasted_iota(jnp.int32, (Lc, Lc), 0))
    b_all = _split_dot_left(tri.astype(BF16), logf, 3)
    c_all = ipre - pltpu.roll(b_all, LANES - H, axis=1)
    c_all_t = c_all.T

    for h in range(H):
        sl = slice(h * HEAD_DIM, (h + 1) * HEAD_DIM)
        qh = _pad_rows(q_ref[:, sl], Lc)
        kh = _pad_rows(k_ref[:, sl], Lc)
        vh = _pad_rows(v_ref[:, sl], Lc)
        bcol = b_all[:, H + h:H + h + 1]
        ccol = c_all[:, h:h + 1]
        crow = c_all_t[h:h + 1, :]
        m_prev = m_ref[h:h + 1, 0:1]
        C = c_ref[h]
        n_row = n_ref[h:h + 1, :]

        logD = jnp.where(tri, bcol + crow, NEG)
        inter = m_prev + bcol
        m_row = jnp.maximum(inter, jnp.max(logD, axis=1, keepdims=True))
        s_qk = _dot_nt(qh, kh) * scale
        w = jnp.exp(logD - m_row) * s_qk
        a_inter = jnp.exp(inter - m_row)
        num = _dot(w.astype(BF16), vh) + a_inter * _dot(qh, C.astype(BF16))
        qn = jnp.sum(qh.astype(F32) * n_row, axis=1, keepdims=True)
        den = jnp.sum(w, axis=1, keepdims=True) + a_inter * qn
        hh = num / jnp.maximum(jnp.abs(den), jnp.exp(-m_row))

        b_last = bcol[Lc - 1:Lc, :]
        gcol = b_last + ccol
        m_new = jnp.maximum(m_prev + b_last, jnp.max(gcol, axis=0, keepdims=True))
        decay = jnp.exp(m_prev + b_last - m_new)
        wg = jnp.exp(gcol - m_new)
        kw = kh.astype(F32) * (scale * wg)
        c_ref[h] = decay * C + _dot(kw.T.astype(BF16), vh)
        n_ref[h:h + 1, :] = decay * n_row + jnp.sum(kw, axis=0, keepdims=True)
        m_ref[h:h + 1, :] = jnp.broadcast_to(m_new, (1, LANES))

        og = _sigmoid(o_ref[:, sl].astype(F32))
        zg = _silu(z_ref[:, sl].astype(F32))
        y_ref[:, sl] = ((_head_norm(hh)[:L] * og) * zg).astype(BF16)


def _split_dot_left(m, x, terms):
    out = None
    r = x
    for t in range(terms):
        p = r.astype(BF16)
        d = _dot(m, p)
        out = d if out is None else out + d
        if t + 1 < terms:
            r = r - p.astype(F32)
    return out


def _mlstm(pg3, if3, bias_if, C0, n0, m0, L):
    B, T, NC = pg3.shape
    H = C0.shape[1]
    W = H * HEAD_DIM
    Lc = max(L, LANES)
    blk = lambda g: pl.BlockSpec((None, L, W), lambda b, c, g=g: (b, c, g))
    st4 = pl.BlockSpec((None, H, HEAD_DIM, HEAD_DIM), lambda b, c: (b, 0, 0, 0))
    st3 = pl.BlockSpec((None, H, LANES), lambda b, c: (b, 0, 0))
    kern = functools.partial(_mlstm_kernel, L=L, Lc=Lc, H=H)
    return pl.pallas_call(
        kern,
        grid=(B, T // L),
        in_specs=[blk(QB), blk(KB), blk(VB), blk(OB), blk(ZB),
                  pl.BlockSpec((None, L, LANES), lambda b, c: (b, c, 0)),
                  pl.BlockSpec((1, LANES), lambda b, c: (0, 0)),
                  st4, st3, st3],
        out_specs=[pl.BlockSpec((None, L, W), lambda b, c: (b, c, 0)), st4, st3, st3],
        out_shape=[jax.ShapeDtypeStruct((B, T, W), BF16),
                   jax.ShapeDtypeStruct((B, H, HEAD_DIM, HEAD_DIM), F32),
                   jax.ShapeDtypeStruct((B, H, LANES), F32),
                   jax.ShapeDtypeStruct((B, H, LANES), F32)],
        compiler_params=_cparams(2),
        name="mlstm",
    )(pg3, pg3, pg3, pg3, pg3, if3, bias_if, C0, n0, m0)


def _ret_kernel(q_ref, k_ref, v_ref, z_ref, cos_ref, sin_ref, dm_ref, qd_ref, kd_ref, s0_ref,
                y_ref, s_ref, *, L, Lc, H, chunk_decay):
    ci = pl.program_id(1)
    scale = HEAD_DIM ** -0.5

    @pl.when(ci == 0)
    def _():
        s_ref[...] = s0_ref[...]

    cos2 = _pad_rows(cos_ref[...], Lc)
    sin2 = _pad_rows(sin_ref[...], Lc)

    def rot(x):
        return x * cos2 + pltpu.roll(x, HEAD_DIM // 2, axis=1) * sin2

    for h in range(H):
        sl = slice(h * HEAD_DIM, (h + 1) * HEAD_DIM)
        qr = rot(_pad_rows(q_ref[:, sl], Lc).astype(F32))
        kr = rot(_pad_rows(k_ref[:, sl], Lc).astype(F32)) * scale
        vh = _pad_rows(v_ref[:, sl], Lc)
        qb = qr.astype(BF16)
        S = s_ref[h]
        att = _dot_nt(qb, kr.astype(BF16)) * dm_ref[h]
        o = _dot(att.astype(BF16), vh) + _dot(qb, S.astype(BF16)) * qd_ref[:, h:h + 1]
        kd = kr * kd_ref[:, h:h + 1]
        s_ref[h] = chunk_decay[h] * S + _dot(kd.T.astype(BF16), vh)
        zg = _silu(z_ref[:, sl].astype(F32))
        y_ref[:, sl] = (_head_norm(o)[:L] * zg).astype(BF16)


def _retention_tables(H, L, Lc):
    lg = np.log1p(-np.exp2(-5.0 - np.arange(H, dtype=np.float64)))
    t = np.arange(Lc, dtype=np.float64)
    rel = t[:, None] - t[None, :]
    dmask = np.where(rel >= 0, np.exp(lg[:, None, None] * np.maximum(rel, 0.0)), 0.0)
    qd = np.zeros((Lc, LANES))
    kd = np.zeros((Lc, LANES))
    qd[:, :H] = np.exp(lg[None, :] * (t[:, None] + 1.0))
    kd[:, :H] = np.where(t[:, None] < L, np.exp(lg[None, :] * (L - 1.0 - t[:, None])), 0.0)
    chunk_decay = tuple(float(v) for v in np.exp(lg * L))
    return (jnp.asarray(dmask, F32), jnp.asarray(qd, F32), jnp.asarray(kd, F32), chunk_decay)


def _retention(pg3, cos2, sin2, S0, L):
    B, T, NC = pg3.shape
    H = S0.shape[1]
    W = H * HEAD_DIM
    Lc = max(L, LANES)
    dmask, qd, kd, chunk_decay = _retention_tables(H, L, Lc)
    blk = lambda g: pl.BlockSpec((None, L, W), lambda b, c, g=g: (b, c, g))
    st4 = pl.BlockSpec((None, H, HEAD_DIM, HEAD_DIM), lambda b, c: (b, 0, 0, 0))
    tab = pl.BlockSpec((L, HEAD_DIM), lambda b, c: (c, 0))
    kern = functools.partial(_ret_kernel, L=L, Lc=Lc, H=H, chunk_decay=chunk_decay)
    return pl.pallas_call(
        kern,
        grid=(B, T // L),
        in_specs=[blk(QC), blk(KC), blk(VC), blk(ZC), tab, tab,
                  pl.BlockSpec((H, Lc, Lc), lambda b, c: (0, 0, 0)),
                  pl.BlockSpec((Lc, LANES), lambda b, c: (0, 0)),
                  pl.BlockSpec((Lc, LANES), lambda b, c: (0, 0)),
                  st4],
        out_specs=[pl.BlockSpec((None, L, W), lambda b, c: (b, c, 0)), st4],
        out_shape=[jax.ShapeDtypeStruct((B, T, W), BF16),
                   jax.ShapeDtypeStruct((B, H, HEAD_DIM, HEAD_DIM), F32)],
        compiler_params=_cparams(2),
        name="retention",
    )(pg3, pg3, pg3, pg3, cos2, sin2, dmask, qd, kd, S0)


def _conv_kernel(u_ref, g_ref, z_ref, c0_ref, w_ref, b_ref, lg_ref, lb_ref, y_ref, c1_ref, xs_ref, cv_ref,
                 *, tt, rows):
    ti = pl.program_id(1)
    hist = CONV_WIDTH - 1
    off = HALO - hist

    @pl.when(ti == 0)
    def _():
        xs_ref[0:HALO, :] = jnp.zeros((HALO, xs_ref.shape[1]), F32)
        xs_ref[off:HALO, :] = c0_ref[...]

    @pl.when(ti > 0)
    def _():
        xs_ref[0:HALO, :] = xs_ref[tt:tt + HALO, :]

    xs_ref[HALO:HALO + tt, :] = u_ref[...].astype(F32) * _sigmoid(g_ref[...].astype(F32))

    W = xs_ref.shape[1]
    for c in range(0, W, LANES):
        for r in range(0, tt, rows):
            acc = jnp.broadcast_to(b_ref[:, c:c + LANES], (rows, LANES))
            for j in range(CONV_WIDTH):
                acc = acc + w_ref[j:j + 1, c:c + LANES] * xs_ref[r + j + off:r + j + off + rows, c:c + LANES]
            cv_ref[r:r + rows, c:c + LANES] = acc

    ln = _head_norm(cv_ref[...]) * lg_ref[...] + lb_ref[...]
    y_ref[...] = (_silu(ln) * _silu(z_ref[...].astype(F32))).astype(BF16)
    c1_ref[...] = xs_ref[tt + off:tt + HALO, :]


def _conv(pg3, conv0, conv_w, conv_b, ln_g, ln_b, tt):
    B, T, NC = pg3.shape
    W = conv0.shape[-1]
    hist = CONV_WIDTH - 1
    rows = min(tt, 128)
    blk = lambda g: pl.BlockSpec((None, tt, W), lambda b, t, g=g: (b, t, g))
    vec = pl.BlockSpec((1, W), lambda b, t: (0, 0))
    kern = functools.partial(_conv_kernel, tt=tt, rows=rows)
    return pl.pallas_call(
        kern,
        grid=(B, T // tt),
        in_specs=[blk(UD), blk(GD), blk(ZD),
                  pl.BlockSpec((None, hist, W), lambda b, t: (b, 0, 0)),
                  pl.BlockSpec((CONV_WIDTH, W), lambda b, t: (0, 0)),
                  vec, vec, vec],
        out_specs=[pl.BlockSpec((None, tt, W), lambda b, t: (b, t, 0)),
                   pl.BlockSpec((None, hist, W), lambda b, t: (b, 0, 0))],
        out_shape=[jax.ShapeDtypeStruct((B, T, W), BF16),
                   jax.ShapeDtypeStruct((B, hist, W), F32)],
        scratch_shapes=[pltpu.VMEM((tt + HALO, W), F32), pltpu.VMEM((tt, W), F32)],
        compiler_params=_cparams(2),
        name="conv",
    )(pg3, pg3, pg3, conv0, conv_w, conv_b.reshape(1, W), ln_g.reshape(1, W), ln_b.reshape(1, W))


def _merge_kernel(ya_ref, yb_ref, yc_ref, yd_ref, g_ref, x_ref, wb_ref, wo_ref, fg_ref, o_ref, mg_ref,
                  *, D, W, final):
    ys = (ya_ref, yb_ref, yc_ref, yd_ref)
    for c in range(0, D, W):
        acc = None
        for i in range(N_BRANCH):
            p = _dot(ys[i][...], wb_ref[i, :, c:c + W])
            t = _sigmoid(g_ref[:, i * D + c:i * D + c + W].astype(F32)) * p
            acc = t if acc is None else acc + t
        mg_ref[:, c:c + W] = acc.astype(BF16)
    o = x_ref[...] + _dot(mg_ref[...], wo_ref[...])
    if final:
        ms = jnp.mean(o * o, axis=-1, keepdims=True)
        o = (o * lax.rsqrt(ms + NORM_EPS)) * fg_ref[...]
    o_ref[...] = o


def _merge(ys, pg2, x2, wb, wo, final_g, tm, final):
    N, D = x2.shape
    W = D // N_BRANCH
    yblk = pl.BlockSpec((tm, W), lambda i: (i, 0))
    kern = functools.partial(_merge_kernel, D=D, W=W, final=final)
    return pl.pallas_call(
        kern,
        grid=(N // tm,),
        in_specs=[yblk, yblk, yblk, yblk,
                  pl.BlockSpec((tm, N_BRANCH * D), lambda i: (i, 1)),
                  pl.BlockSpec((tm, D), lambda i: (i, 0)),
                  pl.BlockSpec((N_BRANCH, W, D), lambda i: (0, 0, 0)),
                  pl.BlockSpec((D, D), lambda i: (0, 0)),
                  pl.BlockSpec((1, D), lambda i: (0, 0))],
        out_specs=pl.BlockSpec((tm, D), lambda i: (i, 0)),
        out_shape=jax.ShapeDtypeStruct((N, D), F32),
        scratch_shapes=[pltpu.VMEM((tm, D), BF16)],
        compiler_params=_cparams(1),
        name="merge",
    )(*ys, pg2, x2, wb, wo, final_g.reshape(1, D))


def _rope_tables(pos):
    half = HEAD_DIM // 2
    inv = ROPE_BASE ** (-jnp.arange(half, dtype=F32) / half)
    ang = pos.astype(F32)[:, None] * inv[None, :]
    cos, sin = jnp.cos(ang), jnp.sin(ang)
    return jnp.concatenate([cos, cos], axis=1), jnp.concatenate([-sin, sin], axis=1)


def _prep_layer(w_in_l, b_i_l, b_f_l, w_branch_l, w_out_l, H):
    D = w_in_l.shape[0]
    W = D // N_BRANCH
    s = (ZB + 1) * W
    w_main = jnp.concatenate([w_in_l[:, :s], w_in_l[:, s + 2 * H:]], axis=1).astype(BF16)
    w_if = jnp.pad(w_in_l[:, s:s + 2 * H], ((0, 0), (0, LANES - 2 * H))).astype(BF16)
    bias_if = jnp.pad(jnp.concatenate([b_i_l, b_f_l]).astype(F32), (0, LANES - 2 * H)).reshape(1, LANES)
    return w_main, w_if, bias_if, w_branch_l.astype(BF16), w_out_l.astype(BF16)


def _trunk(x, past_len, cache_k, cache_v, C0, n0, m0, S0, conv0, layers, conv_w, conv_b, ln_g, ln_b,
           norm_g, final_g, cfg):
    B, T, D = x.shape
    N = B * T
    W = D // N_BRANCH
    H = W // HEAD_DIM
    depth = len(layers)
    cos2, sin2 = _rope_tables(past_len + jnp.arange(T))
    x2 = x.reshape(N, D)
    states = []
    for l in range(depth):
        w_main, w_if, bias_if, wb, wo = layers[l]
        pg, k32, v32, ifp = _inproj(x2, norm_g[l], w_main, w_if, cfg["tm_in"])
        pg3 = pg.reshape(B, T, -1)
        if cache_k is None:
            pk = pv = None
        else:
            pk = cache_k[l].reshape(B, -1, W)
            pv = cache_v[l].reshape(B, -1, W)
        ya = _stick_breaking(pg3, pk, pv, H, cfg["tq"], cfg["tk"])
        m0b = jnp.broadcast_to(m0[l].astype(F32)[..., None], (B, H, LANES))
        yb, C1, n1, m1 = _mlstm(pg3, ifp.reshape(B, T, LANES), bias_if, C0[l].astype(F32),
                                n0[l].astype(F32), m0b, cfg["L"])
        yc, S1 = _retention(pg3, cos2, sin2, S0[l].astype(F32), cfg["L"])
        yd, conv1 = _conv(pg3, conv0[l], conv_w[l], conv_b[l], ln_g[l], ln_b[l], cfg["tt"])
        ys = tuple(y.reshape(N, W) for y in (ya, yb, yc, yd))
        x2 = _merge(ys, pg, x2, wb, wo, final_g, cfg["tm_mg"], final=(l == depth - 1))
        states.append((k32.reshape(B, T, H, HEAD_DIM), v32.reshape(B, T, H, HEAD_DIM),
                       C1, n1, m1[..., 0], S1, conv1))
    stacked = tuple(jnp.stack(s, axis=0) for s in zip(*states))
    return x2.reshape(B, T, D), stacked


def _config(T):
    if T >= 256:
        return dict(tm_in=1024, tq=256, tk=128, L=256, tt=256, tm_mg=256)
    return dict(tm_in=512, tq=T, tk=128, L=T, tt=T, tm_mg=256)


def kernel(x_prompt, x_sample, cache_sb_k, cache_sb_v, state_mlstm_C, state_mlstm_n, state_mlstm_m,
           state_ret_S, state_conv, norm_g, w_in, mlstm_b_i, mlstm_b_f, conv_w, conv_b, conv_ln_g,
           conv_ln_b, w_branch, w_out, final_g):
    depth = w_in.shape[0]
    Bp, Tp, D = x_prompt.shape
    W = D // N_BRANCH
    H = W // HEAD_DIM
    layers = [_prep_layer(w_in[l], mlstm_b_i[l], mlstm_b_f[l], w_branch[l], w_out[l], H) for l in range(depth)]
    zC = jnp.zeros((depth, Bp, H, HEAD_DIM, HEAD_DIM), F32)
    zn = jnp.zeros((depth, Bp, H, HEAD_DIM), F32)
    zm = jnp.zeros((depth, Bp, H), F32)
    zconv = jnp.zeros((depth, Bp, CONV_WIDTH - 1, W), x_prompt.dtype)
    shared = (layers, conv_w, conv_b, conv_ln_g, conv_ln_b, norm_g, final_g)
    y_p, (pk, pv, pC, pn, pm, pS, pconv) = _trunk(
        x_prompt, 0, None, None, zC, zn, zm, zC, zconv, *shared, _config(Tp))
    y_s, (sk, sv, sC, sn, sm, sS, sconv) = _trunk(
        x_sample, cache_sb_k.shape[2], cache_sb_k, cache_sb_v, state_mlstm_C, state_mlstm_n,
        state_mlstm_m, state_ret_S, state_conv, *shared, _config(x_sample.shape[1]))
    return (y_p, y_s, pk, pv, pC, pn, pm, pS, pconv, sk, sv, sC, sn, sm, sS, sconv)
```

```python
import functools

import numpy as np
import jax
import jax.numpy as jnp
from jax import lax
from jax.experimental import pallas as pl
from jax.experimental.pallas import tpu as pltpu

F32 = jnp.float32
BF16 = jnp.bfloat16

HEAD_DIM = 128
N_BRANCH = 4
CONV_WIDTH = 31
ROPE_BASE = 10000.0
NORM_EPS = 1e-6
LN_EPS = 1e-5
LANES = 128
SUBLANES = 8
HALO = 32
NEG = -1e30
SB_CUT = 104.0
VMEM_LIMIT = 56 * 1024 * 1024

QA, KA, VA, ZA, QB, KB, VB, OB, ZB, QC, KC, VC, ZC, UD, GD, ZD = range(16)


def _cparams(n_axes):
    return pltpu.CompilerParams(dimension_semantics=("arbitrary",) * n_axes,
                                vmem_limit_bytes=VMEM_LIMIT)


def _dot(a, b):
    return jnp.dot(a, b, preferred_element_type=F32)


def _dot_nt(a, b):
    return lax.dot_general(a, b, (((1,), (1,)), ((), ())), preferred_element_type=F32)


def _softplus(x):
    return jnp.maximum(x, 0.0) + jnp.log1p(jnp.exp(-jnp.abs(x)))


def _log_sigmoid(x):
    return -_softplus(-x)


def _sigmoid(x):
    return 1.0 / (1.0 + jnp.exp(-x))


def _silu(x):
    return x * _sigmoid(x)


def _head_norm(x):
    mu = jnp.mean(x, axis=-1, keepdims=True)
    xc = x - mu
    return xc * lax.rsqrt(jnp.mean(xc * xc, axis=-1, keepdims=True) + LN_EPS)


def _split_dot(x, m, terms):
    out = None
    r = x
    for t in range(terms):
        p = r.astype(BF16)
        d = _dot(p, m)
        out = d if out is None else out + d
        if t + 1 < terms:
            r = r - p.astype(F32)
    return out


def _pad_rows(x, rows):
    if x.shape[0] == rows:
        return x
    return jnp.concatenate([x, jnp.zeros((rows - x.shape[0],) + x.shape[1:], x.dtype)], axis=0)


def _inproj_kernel(x_ref, g_ref, w_ref, wif_ref, pg_ref, k_ref, v_ref, if_ref, hin_ref, *, W, rows):
    j = pl.program_id(1)
    tm = x_ref.shape[0]

    @pl.when(j == 0)
    def _():
        for r in range(0, tm, rows):
            x = x_ref[r:r + rows, :]
            ms = jnp.mean(x * x, axis=-1, keepdims=True)
            hb = ((x * lax.rsqrt(ms + NORM_EPS)) * g_ref[...]).astype(BF16)
            hin_ref[r:r + rows, :] = hb
            if_ref[r:r + rows, :] = _dot(hb, wif_ref[...])

    acc = _dot(hin_ref[...], w_ref[...])
    pg_ref[...] = acc.astype(BF16)

    @pl.when(j == 0)
    def _():
        k_ref[...] = acc[:, KA * W:(KA + 1) * W]

    @pl.when(j == 1)
    def _():
        v_ref[...] = acc[:, (VA - 2) * W:(VA - 1) * W]


def _inproj(x2, norm_g, w_main, w_if, tm):
    N, D = x2.shape
    NC = w_main.shape[1]
    W = D // N_BRANCH
    tn = 2 * W
    rows = min(tm, 256)
    kern = functools.partial(_inproj_kernel, W=W, rows=rows)
    return pl.pallas_call(
        kern,
        grid=(N // tm, NC // tn),
        in_specs=[
            pl.BlockSpec((tm, D), lambda i, j: (i, 0)),
            pl.BlockSpec((1, D), lambda i, j: (0, 0)),
            pl.BlockSpec((D, tn), lambda i, j: (0, j)),
            pl.BlockSpec((D, LANES), lambda i, j: (0, 0)),
        ],
        out_specs=[
            pl.BlockSpec((tm, tn), lambda i, j: (i, j)),
            pl.BlockSpec((tm, W), lambda i, j: (i, 0)),
            pl.BlockSpec((tm, W), lambda i, j: (i, 0)),
            pl.BlockSpec((tm, LANES), lambda i, j: (i, 0)),
        ],
        out_shape=[
            jax.ShapeDtypeStruct((N, NC), BF16),
            jax.ShapeDtypeStruct((N, W), F32),
            jax.ShapeDtypeStruct((N, W), F32),
            jax.ShapeDtypeStruct((N, LANES), F32),
        ],
        scratch_shapes=[pltpu.VMEM((tm, D), BF16)],
        compiler_params=_cparams(2),
        name="inproj",
    )(x2, norm_g.reshape(1, D), w_main, w_if)


def _sb_kernel(q_ref, kn_ref, vn_ref, z_ref, kp_ref, vp_ref, o_ref, acc_ref, car_ref, *, tq, tk, past0, H):
    qi = pl.program_id(1)
    scale = HEAD_DIM ** -0.5
    tkd = max(tq, LANES)
    heads = [slice(h * HEAD_DIM, (h + 1) * HEAD_DIM) for h in range(H)]

    def strict_upper(n):
        return (lax.broadcasted_iota(jnp.int32, (n, n), 0)
                > lax.broadcasted_iota(jnp.int32, (n, n), 1)).astype(BF16)

    u_new = strict_upper(tkd)
    causal = (lax.broadcasted_iota(jnp.int32, (tq, tkd), 1)
              < lax.broadcasted_iota(jnp.int32, (tq, tkd), 0))
    zs = [_dot_nt(q_ref[:, sl], _pad_rows(kn_ref[:, sl], tkd)) * scale for sl in heads]
    sps = [jnp.where(causal, _softplus(z), 0.0) for z in zs]
    cums = [_split_dot(sp, u_new, 2) for sp in sps]
    ws = [jnp.where(causal, jnp.exp(z - sp - cum), 0.0).astype(BF16) for z, sp, cum in zip(zs, sps, cums)]
    for h, sl in enumerate(heads):
        acc_ref[:, sl] = _dot(ws[h], _pad_rows(vn_ref[:, sl], tkd))
        car_ref[h] = jnp.sum(sps[h], axis=1, keepdims=True)

    u_past = strict_upper(tk)
    n_blocks = (past0 + qi * tq) // tk

    def cond(c):
        kb, go = c
        return jnp.logical_and(kb >= 0, go)

    def body(c):
        kb, _ = c
        start = pl.multiple_of(kb * tk, tk)
        zb = [_dot_nt(q_ref[:, sl], kp_ref[pl.ds(start, tk), sl].astype(BF16)) * scale for sl in heads]
        spb = [_softplus(z) for z in zb]
        cumb = [_split_dot(sp, u_past, 2) for sp in spb]
        cars = [car_ref[h] for h in range(H)]
        wb = [jnp.exp(z - sp - cum - car).astype(BF16) for z, sp, cum, car in zip(zb, spb, cumb, cars)]
        lowest = None
        for h, sl in enumerate(heads):
            acc_ref[:, sl] += _dot(wb[h], vp_ref[pl.ds(start, tk), sl].astype(BF16))
            car_new = cars[h] + jnp.sum(spb[h], axis=1, keepdims=True)
            car_ref[h] = car_new
            lowest = car_new if lowest is None else jnp.minimum(lowest, car_new)
        return kb - 1, jnp.min(lowest) < SB_CUT

    lax.while_loop(cond, body, (n_blocks - 1, True))

    za = z_ref[...].astype(F32)
    o_ref[...] = (acc_ref[...] * _silu(za)).astype(BF16)


def _stick_breaking(pg3, cache_k, cache_v, layer, H, tq, tk):
    B, T, NC = pg3.shape
    W = H * HEAD_DIM
    if cache_k is None:
        kp, vp, past0 = pg3, pg3, 0
        kp_spec = pl.BlockSpec((None, T, W), lambda b, i: (b, 0, KA), pipeline_mode=pl.Buffered(1))
        vp_spec = pl.BlockSpec((None, T, W), lambda b, i: (b, 0, VA), pipeline_mode=pl.Buffered(1))
    else:
        kp, vp, past0 = cache_k, cache_v, cache_k.shape[2]
        kp_spec = pl.BlockSpec((None, None, past0, W), lambda b, i: (layer, b, 0, 0))
        vp_spec = pl.BlockSpec((None, None, past0, W), lambda b, i: (layer, b, 0, 0))
    blk = lambda g: pl.BlockSpec((None, tq, W), lambda b, i, g=g: (b, i, g))
    kern = functools.partial(_sb_kernel, tq=tq, tk=tk, past0=past0, H=H)
    return pl.pallas_call(
        kern,
        grid=(B, T // tq),
        in_specs=[blk(QA), blk(KA), blk(VA), blk(ZA), kp_spec, vp_spec],
        out_specs=pl.BlockSpec((None, tq, W), lambda b, i: (b, i, 0)),
        out_shape=jax.ShapeDtypeStruct((B, T, W), BF16),
        scratch_shapes=[pltpu.VMEM((tq, W), F32), pltpu.VMEM((H, tq, 1), F32)],
        compiler_params=_cparams(2),
        name="stick_breaking",
    )(pg3, pg3, pg3, pg3, kp, vp)


def _mlstm_kernel(q_ref, k_ref, v_ref, o_ref, z_ref, if_ref, bias_ref, c0_ref, n0_ref, m0_ref,
                  y_ref, c_ref, n_ref, m_ref, *, L, Lc, H):
    ci = pl.program_id(1)
    scale = HEAD_DIM ** -0.5

    @pl.when(ci == 0)
    def _():
        c_ref[...] = c0_ref[...]
        n_ref[...] = n0_ref[...]
        m_ref[...] = m0_ref[...]

    ifv = _pad_rows(if_ref[...] + bias_ref[...], Lc)
    valid = lax.broadcasted_iota(jnp.int32, (Lc, LANES), 0) < L
    ipre = jnp.where(valid, ifv, NEG)
    logf = jnp.where(valid, _log_sigmoid(ifv), 0.0)
    tri = (lax.broadcasted_iota(jnp.int32, (Lc, Lc), 1)
           <= lax.broadcasted_iota(jnp.int32, (Lc, Lc), 0))
    b_all = _split_dot_left(tri.astype(BF16), logf, 3)
    c_all = ipre - pltpu.roll(b_all, LANES - H, axis=1)
    c_all_t = c_all.T

    hs = range(H)
    heads = [slice(h * HEAD_DIM, (h + 1) * HEAD_DIM) for h in hs]
    qh = [_pad_rows(q_ref[:, sl], Lc) for sl in heads]
    kh = [_pad_rows(k_ref[:, sl], Lc) for sl in heads]
    vh = [_pad_rows(v_ref[:, sl], Lc) for sl in heads]
    C = [c_ref[h] for h in hs]
    n_row = [n_ref[h:h + 1, :] for h in hs]
    m_prev = [m_ref[h:h + 1, 0:1] for h in hs]
    s_qk = [_dot_nt(qh[h], kh[h]) * scale for h in hs]
    q_c = [_dot(qh[h], C[h].astype(BF16)) for h in hs]

    bcol = [b_all[:, H + h:H + h + 1] for h in hs]
    b_last = [bcol[h][Lc - 1:Lc, :] for h in hs]
    gcol = [b_last[h] + c_all[:, h:h + 1] for h in hs]
    m_new = [jnp.maximum(m_prev[h] + b_last[h], jnp.max(gcol[h], axis=0, keepdims=True)) for h in hs]
    decay = [jnp.exp(m_prev[h] + b_last[h] - m_new[h]) for h in hs]
    kw = [kh[h].astype(F32) * (scale * jnp.exp(gcol[h] - m_new[h])) for h in hs]
    kw_v = [_dot(kw[h].T.astype(BF16), vh[h]) for h in hs]
    for h in hs:
        c_ref[h] = decay[h] * C[h] + kw_v[h]
        n_ref[h:h + 1, :] = decay[h] * n_row[h] + jnp.sum(kw[h], axis=0, keepdims=True)
        m_ref[h:h + 1, :] = jnp.broadcast_to(m_new[h], (1, LANES))

    logD = [jnp.where(tri, bcol[h] + c_all_t[h:h + 1, :], NEG) for h in hs]
    inter = [m_prev[h] + bcol[h] for h in hs]
    m_row = [jnp.maximum(inter[h], jnp.max(logD[h], axis=1, keepdims=True)) for h in hs]
    w = [jnp.exp(logD[h] - m_row[h]) * s_qk[h] for h in hs]
    w_v = [_dot(w[h].astype(BF16), vh[h]) for h in hs]
    for h, sl in enumerate(heads):
        a_inter = jnp.exp(inter[h] - m_row[h])
        num = w_v[h] + a_inter * q_c[h]
        qn = jnp.sum(qh[h].astype(F32) * n_row[h], axis=1, keepdims=True)
        den = jnp.sum(w[h], axis=1, keepdims=True) + a_inter * qn
        hh = num / jnp.maximum(jnp.abs(den), jnp.exp(-m_row[h]))
        og = _sigmoid(o_ref[:, sl].astype(F32))
        zg = _silu(z_ref[:, sl].astype(F32))
        y_ref[:, sl] = ((_head_norm(hh)[:L] * og) * zg).astype(BF16)


def _split_dot_left(m, x, terms):
    out = None
    r = x
    for t in range(terms):
        p = r.astype(BF16)
        d = _dot(m, p)
        out = d if out is None else out + d
        if t + 1 < terms:
            r = r - p.astype(F32)
    return out


def _mlstm(pg3, if3, bias_if, C0, n0, m0, L):
    B, T, NC = pg3.shape
    H = C0.shape[1]
    W = H * HEAD_DIM
    Lc = max(L, LANES)
    blk = lambda g: pl.BlockSpec((None, L, W), lambda b, c, g=g: (b, c, g))
    st4 = pl.BlockSpec((None, H, HEAD_DIM, HEAD_DIM), lambda b, c: (b, 0, 0, 0))
    st3 = pl.BlockSpec((None, H, LANES), lambda b, c: (b, 0, 0))
    kern = functools.partial(_mlstm_kernel, L=L, Lc=Lc, H=H)
    return pl.pallas_call(
        kern,
        grid=(B, T // L),
        in_specs=[blk(QB), blk(KB), blk(VB), blk(OB), blk(ZB),
                  pl.BlockSpec((None, L, LANES), lambda b, c: (b, c, 0)),
                  pl.BlockSpec((1, LANES), lambda b, c: (0, 0)),
                  st4, st3, st3],
        out_specs=[pl.BlockSpec((None, L, W), lambda b, c: (b, c, 0)), st4, st3, st3],
        out_shape=[jax.ShapeDtypeStruct((B, T, W), BF16),
                   jax.ShapeDtypeStruct((B, H, HEAD_DIM, HEAD_DIM), F32),
                   jax.ShapeDtypeStruct((B, H, LANES), F32),
                   jax.ShapeDtypeStruct((B, H, LANES), F32)],
        compiler_params=_cparams(2),
        name="mlstm",
    )(pg3, pg3, pg3, pg3, pg3, if3, bias_if, C0, n0, m0)


def _ret_kernel(q_ref, k_ref, v_ref, z_ref, cos_ref, sin_ref, dm_ref, qd_ref, kd_ref, s0_ref,
                y_ref, s_ref, *, L, Lc, H, chunk_decay):
    ci = pl.program_id(1)
    scale = HEAD_DIM ** -0.5

    @pl.when(ci == 0)
    def _():
        s_ref[...] = s0_ref[...]

    cos2 = _pad_rows(cos_ref[...], Lc)
    sin2 = _pad_rows(sin_ref[...], Lc)

    def rot(x):
        return x * cos2 + pltpu.roll(x, HEAD_DIM // 2, axis=1) * sin2

    hs = range(H)
    heads = [slice(h * HEAD_DIM, (h + 1) * HEAD_DIM) for h in hs]
    qb = [rot(_pad_rows(q_ref[:, sl], Lc).astype(F32)).astype(BF16) for sl in heads]
    kr = [rot(_pad_rows(k_ref[:, sl], Lc).astype(F32)) * scale for sl in heads]
    vh = [_pad_rows(v_ref[:, sl], Lc) for sl in heads]
    S = [s_ref[h] for h in hs]
    qk = [_dot_nt(qb[h], kr[h].astype(BF16)) for h in hs]
    q_s = [_dot(qb[h], S[h].astype(BF16)) for h in hs]
    kd_v = [_dot((kr[h] * kd_ref[:, h:h + 1]).T.astype(BF16), vh[h]) for h in hs]
    for h in hs:
        s_ref[h] = chunk_decay[h] * S[h] + kd_v[h]
    att_v = [_dot((qk[h] * dm_ref[h]).astype(BF16), vh[h]) for h in hs]
    for h, sl in enumerate(heads):
        o = att_v[h] + q_s[h] * qd_ref[:, h:h + 1]
        zg = _silu(z_ref[:, sl].astype(F32))
        y_ref[:, sl] = (_head_norm(o)[:L] * zg).astype(BF16)


def _retention_tables(H, L, Lc):
    lg = np.log1p(-np.exp2(-5.0 - np.arange(H, dtype=np.float64)))
    t = np.arange(Lc, dtype=np.float64)
    rel = t[:, None] - t[None, :]
    dmask = np.where(rel >= 0, np.exp(lg[:, None, None] * np.maximum(rel, 0.0)), 0.0)
    qd = np.zeros((Lc, LANES))
    kd = np.zeros((Lc, LANES))
    qd[:, :H] = np.exp(lg[None, :] * (t[:, None] + 1.0))
    kd[:, :H] = np.where(t[:, None] < L, np.exp(lg[None, :] * (L - 1.0 - t[:, None])), 0.0)
    chunk_decay = tuple(float(v) for v in np.exp(lg * L))
    return (jnp.asarray(dmask, F32), jnp.asarray(qd, F32), jnp.asarray(kd, F32), chunk_decay)


def _retention(pg3, cos2, sin2, S0, L):
    B, T, NC = pg3.shape
    H = S0.shape[1]
    W = H * HEAD_DIM
    Lc = max(L, LANES)
    dmask, qd, kd, chunk_decay = _retention_tables(H, L, Lc)
    blk = lambda g: pl.BlockSpec((None, L, W), lambda b, c, g=g: (b, c, g))
    st4 = pl.BlockSpec((None, H, HEAD_DIM, HEAD_DIM), lambda b, c: (b, 0, 0, 0))
    tab = pl.BlockSpec((L, HEAD_DIM), lambda b, c: (c, 0))
    kern = functools.partial(_ret_kernel, L=L, Lc=Lc, H=H, chunk_decay=chunk_decay)
    return pl.pallas_call(
        kern,
        grid=(B, T // L),
        in_specs=[blk(QC), blk(KC), blk(VC), blk(ZC), tab, tab,
                  pl.BlockSpec((H, Lc, Lc), lambda b, c: (0, 0, 0)),
                  pl.BlockSpec((Lc, LANES), lambda b, c: (0, 0)),
                  pl.BlockSpec((Lc, LANES), lambda b, c: (0, 0)),
                  st4],
        out_specs=[pl.BlockSpec((None, L, W), lambda b, c: (b, c, 0)), st4],
        out_shape=[jax.ShapeDtypeStruct((B, T, W), BF16),
                   jax.ShapeDtypeStruct((B, H, HEAD_DIM, HEAD_DIM), F32)],
        compiler_params=_cparams(2),
        name="retention",
    )(pg3, pg3, pg3, pg3, cos2, sin2, dmask, qd, kd, S0)


def _conv_kernel(u_ref, g_ref, z_ref, c0_ref, w_ref, b_ref, lg_ref, lb_ref, y_ref, c1_ref, xs_ref, cv_ref,
                 sh_ref, *, tt, rows):
    ti = pl.program_id(1)
    hist = CONV_WIDTH - 1
    off = HALO - hist

    @pl.when(ti == 0)
    def _():
        xs_ref[0:HALO, :] = jnp.zeros((HALO, xs_ref.shape[1]), F32)
        xs_ref[off:HALO, :] = c0_ref[...]

    @pl.when(ti > 0)
    def _():
        xs_ref[0:HALO, :] = xs_ref[tt:tt + HALO, :]

    xs_ref[HALO:HALO + tt, :] = u_ref[...].astype(F32) * _sigmoid(g_ref[...].astype(F32))

    for s in range(SUBLANES):
        span = tt + (CONV_WIDTH - 1 - s) // SUBLANES * SUBLANES
        sh_ref[s, 0:span, :] = xs_ref[off + s:off + s + span, :]

    W = xs_ref.shape[1]
    for c in range(0, W, LANES):
        for r in range(0, tt, rows):
            acc = jnp.broadcast_to(b_ref[:, c:c + LANES], (rows, LANES))
            for j in range(CONV_WIDTH):
                a, s = j // SUBLANES * SUBLANES, j % SUBLANES
                acc = acc + w_ref[j:j + 1, c:c + LANES] * sh_ref[s, r + a:r + a + rows, c:c + LANES]
            cv_ref[r:r + rows, c:c + LANES] = acc

    ln = _head_norm(cv_ref[...]) * lg_ref[...] + lb_ref[...]
    y_ref[...] = (_silu(ln) * _silu(z_ref[...].astype(F32))).astype(BF16)
    c1_ref[...] = xs_ref[tt + off:tt + HALO, :]


def _conv(pg3, conv0, conv_w, conv_b, ln_g, ln_b, tt):
    B, T, NC = pg3.shape
    W = conv0.shape[-1]
    hist = CONV_WIDTH - 1
    rows = min(tt, 128)
    blk = lambda g: pl.BlockSpec((None, tt, W), lambda b, t, g=g: (b, t, g))
    vec = pl.BlockSpec((1, W), lambda b, t: (0, 0))
    kern = functools.partial(_conv_kernel, tt=tt, rows=rows)
    return pl.pallas_call(
        kern,
        grid=(B, T // tt),
        in_specs=[blk(UD), blk(GD), blk(ZD),
                  pl.BlockSpec((None, hist, W), lambda b, t: (b, 0, 0)),
                  pl.BlockSpec((CONV_WIDTH, W), lambda b, t: (0, 0)),
                  vec, vec, vec],
        out_specs=[pl.BlockSpec((None, tt, W), lambda b, t: (b, t, 0)),
                   pl.BlockSpec((None, hist, W), lambda b, t: (b, 0, 0))],
        out_shape=[jax.ShapeDtypeStruct((B, T, W), BF16),
                   jax.ShapeDtypeStruct((B, hist, W), F32)],
        scratch_shapes=[pltpu.VMEM((tt + HALO, W), F32), pltpu.VMEM((tt, W), F32),
                        pltpu.VMEM((SUBLANES, tt + HALO - SUBLANES, W), F32)],
        compiler_params=_cparams(2),
        name="conv",
    )(pg3, pg3, pg3, conv0, conv_w, conv_b.reshape(1, W), ln_g.reshape(1, W), ln_b.reshape(1, W))


def _merge_kernel(ya_ref, yb_ref, yc_ref, yd_ref, g_ref, x_ref, wb_ref, wo_ref, fg_ref, o_ref, mg_ref,
                  *, D, W, final):
    ys = (ya_ref, yb_ref, yc_ref, yd_ref)
    for c in range(0, D, W):
        acc = None
        for i in range(N_BRANCH):
            p = _dot(ys[i][...], wb_ref[i, :, c:c + W])
            t = _sigmoid(g_ref[:, i * D + c:i * D + c + W].astype(F32)) * p
            acc = t if acc is None else acc + t
        mg_ref[:, c:c + W] = acc.astype(BF16)
    o = x_ref[...] + _dot(mg_ref[...], wo_ref[...])
    if final:
        ms = jnp.mean(o * o, axis=-1, keepdims=True)
        o = (o * lax.rsqrt(ms + NORM_EPS)) * fg_ref[...]
    o_ref[...] = o


def _merge(ys, pg2, x2, wb, wo, final_g, tm, final):
    N, D = x2.shape
    W = D // N_BRANCH
    yblk = pl.BlockSpec((tm, W), lambda i: (i, 0))
    kern = functools.partial(_merge_kernel, D=D, W=W, final=final)
    return pl.pallas_call(
        kern,
        grid=(N // tm,),
        in_specs=[yblk, yblk, yblk, yblk,
                  pl.BlockSpec((tm, N_BRANCH * D), lambda i: (i, 1)),
                  pl.BlockSpec((tm, D), lambda i: (i, 0)),
                  pl.BlockSpec((N_BRANCH, W, D), lambda i: (0, 0, 0)),
                  pl.BlockSpec((D, D), lambda i: (0, 0)),
                  pl.BlockSpec((1, D), lambda i: (0, 0))],
        out_specs=pl.BlockSpec((tm, D), lambda i: (i, 0)),
        out_shape=jax.ShapeDtypeStruct((N, D), F32),
        scratch_shapes=[pltpu.VMEM((tm, D), BF16)],
        compiler_params=_cparams(1),
        name="merge",
    )(*ys, pg2, x2, wb, wo, final_g.reshape(1, D))


def _rope_tables(pos):
    half = HEAD_DIM // 2
    inv = ROPE_BASE ** (-jnp.arange(half, dtype=F32) / half)
    ang = pos.astype(F32)[:, None] * inv[None, :]
    cos, sin = jnp.cos(ang), jnp.sin(ang)
    return jnp.concatenate([cos, cos], axis=1), jnp.concatenate([-sin, sin], axis=1)


def _wprep_kernel(a_ref, b_ref, g_ref, o_ref, og_ref, *, n_gate, plain_blocks, plain_lanes, rows):
    j = pl.program_id(0)
    D, tn = a_ref.shape
    nl = tn // LANES
    lane = lax.broadcasted_iota(jnp.int32, (rows, LANES), 1)

    def src(r, c):
        if c < nl:
            return a_ref[r:r + rows, c * LANES:(c + 1) * LANES]
        return b_ref[r:r + rows, :]

    def copy(first_shifted):
        for r in range(0, D, rows):
            rolled = {c: pltpu.roll(src(r, c), LANES - n_gate, 1) for c in range(first_shifted, nl + 1)}
            for c in range(nl):
                if c < first_shifted:
                    v = src(r, c)
                else:
                    v = jnp.where(lane < LANES - n_gate, rolled[c], rolled[c + 1])
                o_ref[r:r + rows, c * LANES:(c + 1) * LANES] = v.astype(BF16)

    @pl.when(j < plain_blocks)
    def _():
        copy(nl)

    @pl.when(j == plain_blocks)
    def _():
        copy(plain_lanes)

    @pl.when(j > plain_blocks)
    def _():
        copy(0)

    @pl.when(j == 0)
    def _():
        for r in range(0, D, rows):
            og_ref[r:r + rows, :] = jnp.where(lane < n_gate, g_ref[r:r + rows, :], 0.0).astype(BF16)


def _wprep(w_in, layer, H):
    _, D, n_in = w_in.shape
    W = D // N_BRANCH
    n_gate = 2 * H
    s = (ZB + 1) * W
    NC = n_in - n_gate
    tn = 2 * W
    kern = functools.partial(_wprep_kernel, n_gate=n_gate, plain_blocks=s // tn,
                             plain_lanes=(s % tn) // LANES, rows=256)
    return pl.pallas_call(
        kern,
        grid=(NC // tn,),
        in_specs=[pl.BlockSpec((None, D, tn), lambda j: (layer, 0, j)),
                  pl.BlockSpec((None, D, LANES), lambda j: (layer, 0, (j + 1) * (tn // LANES))),
                  pl.BlockSpec((None, D, LANES), lambda j: (layer, 0, s // LANES))],
        out_specs=[pl.BlockSpec((D, tn), lambda j: (0, j)),
                   pl.BlockSpec((D, LANES), lambda j: (0, 0))],
        out_shape=[jax.ShapeDtypeStruct((D, NC), BF16), jax.ShapeDtypeStruct((D, LANES), BF16)],
        compiler_params=_cparams(1),
        name="wprep",
    )(w_in, w_in, w_in)


def _prep_layer(w_in, layer, b_i_l, b_f_l, w_branch_l, w_out_l, H):
    w_main, w_if = _wprep(w_in, layer, H)
    bias_if = jnp.pad(jnp.concatenate([b_i_l, b_f_l]).astype(F32), (0, LANES - 2 * H)).reshape(1, LANES)
    return w_main, w_if, bias_if, w_branch_l.astype(BF16), w_out_l.astype(BF16)


def _trunk(x, past_len, cache_k, cache_v, C0, n0, m0, S0, conv0, layers, conv_w, conv_b, ln_g, ln_b,
           norm_g, final_g, cfg):
    B, T, D = x.shape
    N = B * T
    W = D // N_BRANCH
    H = W // HEAD_DIM
    depth = len(layers)
    cos2, sin2 = _rope_tables(past_len + jnp.arange(T))
    x2 = x.reshape(N, D)
    if cache_k is None:
        ck4 = cv4 = None
    else:
        ck4 = cache_k.reshape(depth, B, -1, W)
        cv4 = cache_v.reshape(depth, B, -1, W)
    states = []
    for l in range(depth):
        w_main, w_if, bias_if, wb, wo = layers[l]
        pg, k32, v32, ifp = _inproj(x2, norm_g[l], w_main, w_if, cfg["tm_in"])
        pg3 = pg.reshape(B, T, -1)
        ya = _stick_breaking(pg3, ck4, cv4, l, H, cfg["tq"], cfg["tk"])
        m0b = jnp.broadcast_to(m0[l].astype(F32)[..., None], (B, H, LANES))
        yb, C1, n1, m1 = _mlstm(pg3, ifp.reshape(B, T, LANES), bias_if, C0[l].astype(F32),
                                n0[l].astype(F32), m0b, cfg["L"])
        yc, S1 = _retention(pg3, cos2, sin2, S0[l].astype(F32), cfg["L"])
        yd, conv1 = _conv(pg3, conv0[l], conv_w[l], conv_b[l], ln_g[l], ln_b[l], cfg["tt"])
        ys = tuple(y.reshape(N, W) for y in (ya, yb, yc, yd))
        x2 = _merge(ys, pg, x2, wb, wo, final_g, cfg["tm_mg"], final=(l == depth - 1))
        states.append((k32.reshape(B, T, H, HEAD_DIM), v32.reshape(B, T, H, HEAD_DIM),
                       C1, n1, m1[..., 0], S1, conv1))
    stacked = tuple(jnp.stack(s, axis=0) for s in zip(*states))
    return x2.reshape(B, T, D), stacked


def _config(T):
    if T >= 256:
        return dict(tm_in=1024, tq=256, tk=256, L=256, tt=256, tm_mg=256)
    return dict(tm_in=512, tq=T, tk=128, L=T, tt=T, tm_mg=256)


def kernel(x_prompt, x_sample, cache_sb_k, cache_sb_v, state_mlstm_C, state_mlstm_n, state_mlstm_m,
           state_ret_S, state_conv, norm_g, w_in, mlstm_b_i, mlstm_b_f, conv_w, conv_b, conv_ln_g,
           conv_ln_b, w_branch, w_out, final_g):
    depth = w_in.shape[0]
    Bp, Tp, D = x_prompt.shape
    W = D // N_BRANCH
    H = W // HEAD_DIM
    layers = [_prep_layer(w_in, l, mlstm_b_i[l], mlstm_b_f[l], w_branch[l], w_out[l], H) for l in range(depth)]
    zC = jnp.zeros((depth, Bp, H, HEAD_DIM, HEAD_DIM), F32)
    zn = jnp.zeros((depth, Bp, H, HEAD_DIM), F32)
    zm = jnp.zeros((depth, Bp, H), F32)
    zconv = jnp.zeros((depth, Bp, CONV_WIDTH - 1, W), x_prompt.dtype)
    shared = (layers, conv_w, conv_b, conv_ln_g, conv_ln_b, norm_g, final_g)
    y_p, (pk, pv, pC, pn, pm, pS, pconv) = _trunk(
        x_prompt, 0, None, None, zC, zn, zm, zC, zconv, *shared, _config(Tp))
    y_s, (sk, sv, sC, sn, sm, sS, sconv) = _trunk(
        x_sample, cache_sb_k.shape[2], cache_sb_k, cache_sb_v, state_mlstm_C, state_mlstm_n,
        state_mlstm_m, state_ret_S, state_conv, *shared, _config(x_sample.shape[1]))
    return (y_p, y_s, pk, pv, pC, pn, pm, pS, pconv, sk, sv, sC, sn, sm, sS, sconv)
```

```python
import functools

import numpy as np
import jax
import jax.numpy as jnp
from jax import lax
from jax.experimental import pallas as pl
from jax.experimental.pallas import tpu as pltpu

F32 = jnp.float32
BF16 = jnp.bfloat16

HEAD_DIM = 128
N_BRANCH = 4
CONV_WIDTH = 31
ROPE_BASE = 10000.0
NORM_EPS = 1e-6
LN_EPS = 1e-5
LANES = 128
SUBLANES = 8
HALO = 32
NEG = -1e30
SB_CUT = 104.0
VMEM_LIMIT = 56 * 1024 * 1024

QA, KA, VA, ZA, QB, KB, VB, OB, ZB, QC, KC, VC, ZC, UD, GD, ZD = range(16)


def _cparams(n_axes):
    return pltpu.CompilerParams(dimension_semantics=("arbitrary",) * n_axes,
                                vmem_limit_bytes=VMEM_LIMIT)


def _dot(a, b):
    return jnp.dot(a, b, preferred_element_type=F32)


def _dot_nt(a, b):
    return lax.dot_general(a, b, (((1,), (1,)), ((), ())), preferred_element_type=F32)


def _dot_tn(a, b):
    return lax.dot_general(a, b, (((0,), (0,)), ((), ())), preferred_element_type=F32)


def _softplus(x):
    return jnp.maximum(x, 0.0) + jnp.log1p(jnp.exp(-jnp.abs(x)))


def _log_sigmoid(x):
    return -_softplus(-x)


def _sigmoid(x):
    return 1.0 / (1.0 + jnp.exp(-x))


def _silu(x):
    return x * _sigmoid(x)


def _head_norm(x):
    mu = jnp.mean(x, axis=-1, keepdims=True)
    xc = x - mu
    return xc * lax.rsqrt(jnp.mean(xc * xc, axis=-1, keepdims=True) + LN_EPS)


def _split_dot(x, m, terms):
    out = None
    r = x
    for t in range(terms):
        p = r.astype(BF16)
        d = _dot(p, m)
        out = d if out is None else out + d
        if t + 1 < terms:
            r = r - p.astype(F32)
    return out


def _pad_rows(x, rows):
    if x.shape[0] == rows:
        return x
    return jnp.concatenate([x, jnp.zeros((rows - x.shape[0],) + x.shape[1:], x.dtype)], axis=0)


def _inproj_kernel(x_ref, g_ref, w_ref, wif_ref, kprev_ref, vprev_ref, pg_ref, k_ref, v_ref, if_ref, hin_ref,
                   *, W, H, rows):
    del kprev_ref, vprev_ref
    j = pl.program_id(1)
    tm = x_ref.shape[0]

    @pl.when(j == 0)
    def _():
        for r in range(0, tm, rows):
            x = x_ref[r:r + rows, :]
            ms = jnp.mean(x * x, axis=-1, keepdims=True)
            hb = ((x * lax.rsqrt(ms + NORM_EPS)) * g_ref[...]).astype(BF16)
            hin_ref[r:r + rows, :] = hb
            if_ref[r:r + rows, :] = _dot_nt(hb, wif_ref[...])

    acc = _dot_nt(hin_ref[...], w_ref[...])
    pg_ref[...] = acc.astype(BF16)

    def put_state(ref, col0):
        for h in range(H):
            ref[pl.ds(h, tm, stride=H), :] = acc[:, col0 + h * HEAD_DIM:col0 + (h + 1) * HEAD_DIM]

    @pl.when(j == 0)
    def _():
        put_state(k_ref, KA * W)

    @pl.when(j == 1)
    def _():
        put_state(v_ref, (VA - 2) * W)


def _inproj(x2, norm_g, wt_main, wt_if, kv_prev, layer, depth, H, tm):
    N, D = x2.shape
    NC = wt_main.shape[0]
    W = D // N_BRANCH
    tn = 2 * W
    rows = min(tm, 256)
    kern = functools.partial(_inproj_kernel, W=W, H=H, rows=rows)
    kv_spec = pl.BlockSpec((None, tm * H, HEAD_DIM), lambda i, j: (layer, i, 0))
    kv_shape = jax.ShapeDtypeStruct((depth, N * H, HEAD_DIM), F32)
    if kv_prev is None:
        kv_prev = (jnp.zeros(kv_shape.shape, F32), jnp.zeros(kv_shape.shape, F32))
    return pl.pallas_call(
        kern,
        grid=(N // tm, NC // tn),
        in_specs=[
            pl.BlockSpec((tm, D), lambda i, j: (i, 0)),
            pl.BlockSpec((1, D), lambda i, j: (0, 0)),
            pl.BlockSpec((tn, D), lambda i, j: (j, 0)),
            pl.BlockSpec((LANES, D), lambda i, j: (0, 0)),
            pl.BlockSpec(memory_space=pl.ANY),
            pl.BlockSpec(memory_space=pl.ANY),
        ],
        out_specs=[
            pl.BlockSpec((tm, tn), lambda i, j: (i, j)),
            kv_spec,
            kv_spec,
            pl.BlockSpec((tm, LANES), lambda i, j: (i, 0)),
        ],
        out_shape=[jax.ShapeDtypeStruct((N, NC), BF16), kv_shape, kv_shape,
                   jax.ShapeDtypeStruct((N, LANES), F32)],
        scratch_shapes=[pltpu.VMEM((tm, D), BF16)],
        input_output_aliases={4: 1, 5: 2},
        compiler_params=_cparams(2),
        name="inproj",
    )(x2, norm_g.reshape(1, D), wt_main, wt_if, *kv_prev)


SB_SCALE = HEAD_DIM ** -0.5


def _strict_upper(n):
    return (lax.broadcasted_iota(jnp.int32, (n, n), 0)
            > lax.broadcasted_iota(jnp.int32, (n, n), 1)).astype(BF16)


def _sb_new_rows(q_ref, kn_ref, vn_ref, acc_ref, car_ref, heads, tq):
    tkd = max(tq, LANES)
    u_new = _strict_upper(tkd)
    causal = (lax.broadcasted_iota(jnp.int32, (tq, tkd), 1)
              < lax.broadcasted_iota(jnp.int32, (tq, tkd), 0))
    zs = [_dot_nt(q_ref[:, sl], _pad_rows(kn_ref[:, sl], tkd)) * SB_SCALE for sl in heads]
    sps = [jnp.where(causal, _softplus(z), 0.0) for z in zs]
    cums = [_split_dot(sp, u_new, 2) for sp in sps]
    ws = [jnp.where(causal, jnp.exp(z - sp - cum), 0.0).astype(BF16) for z, sp, cum in zip(zs, sps, cums)]
    for h, sl in enumerate(heads):
        acc_ref[:, sl] = _dot(ws[h], _pad_rows(vn_ref[:, sl], tkd))
        car_ref[h] = jnp.sum(sps[h], axis=1, keepdims=True)


def _sb_past_block(q_ref, k_of, v_of, acc_ref, car_ref, heads, u_past):
    zb = [_dot_nt(q_ref[:, sl], k_of(h)) * SB_SCALE for h, sl in enumerate(heads)]
    spb = [_softplus(z) for z in zb]
    cumb = [_split_dot(sp, u_past, 2) for sp in spb]
    cars = [car_ref[h] for h in range(len(heads))]
    wb = [jnp.exp(z - sp - cum - car).astype(BF16) for z, sp, cum, car in zip(zb, spb, cumb, cars)]
    lowest = None
    for h, sl in enumerate(heads):
        acc_ref[:, sl] += _dot(wb[h], v_of(h))
        car_new = cars[h] + jnp.sum(spb[h], axis=1, keepdims=True)
        car_ref[h] = car_new
        lowest = car_new if lowest is None else jnp.minimum(lowest, car_new)
    return jnp.min(lowest)


def _sb_kernel(q_ref, kn_ref, vn_ref, z_ref, kp_ref, vp_ref, o_ref, acc_ref, car_ref, *, tq, tk, H):
    qi = pl.program_id(1)
    heads = [slice(h * HEAD_DIM, (h + 1) * HEAD_DIM) for h in range(H)]
    _sb_new_rows(q_ref, kn_ref, vn_ref, acc_ref, car_ref, heads, tq)
    u_past = _strict_upper(tk)

    def cond(c):
        kb, go = c
        return jnp.logical_and(kb >= 0, go)

    def body(c):
        kb, _ = c
        start = pl.multiple_of(kb * tk, tk)
        lowest = _sb_past_block(q_ref, lambda h: kp_ref[pl.ds(start, tk), heads[h]],
                                lambda h: vp_ref[pl.ds(start, tk), heads[h]], acc_ref, car_ref, heads, u_past)
        return kb - 1, lowest < SB_CUT

    lax.while_loop(cond, body, ((qi * tq) // tk - 1, True))
    o_ref[...] = (acc_ref[...] * _silu(z_ref[...].astype(F32))).astype(BF16)


def _sb_cache_kernel(q_ref, kn_ref, vn_ref, z_ref, kc_ref, vc_ref, o_ref, acc_ref, car_ref, kbuf, vbuf, sem,
                     *, tq, tk, H, layer, n_blocks):
    b = pl.program_id(0)
    heads = [slice(h * HEAD_DIM, (h + 1) * HEAD_DIM) for h in range(H)]
    rows = tk * H

    def copies(i):
        src = pl.ds((n_blocks - 1 - i) * rows, rows)
        slot = i % 2
        return (pltpu.make_async_copy(kc_ref.at[layer, b, src, :], kbuf.at[slot], sem.at[0, slot]),
                pltpu.make_async_copy(vc_ref.at[layer, b, src, :], vbuf.at[slot], sem.at[1, slot]))

    for c in copies(0):
        c.start()
    _sb_new_rows(q_ref, kn_ref, vn_ref, acc_ref, car_ref, heads, tq)
    u_past = _strict_upper(tk)

    def cond(c):
        i, go = c
        return jnp.logical_and(i < n_blocks, go)

    def body(c):
        i, _ = c
        slot = i % 2
        for cp in copies(i):
            cp.wait()

        @pl.when(i + 1 < n_blocks)
        def _():
            for cp in copies(i + 1):
                cp.start()

        lowest = _sb_past_block(q_ref, lambda h: kbuf[slot, pl.ds(h, tk, stride=H), :].astype(BF16),
                                lambda h: vbuf[slot, pl.ds(h, tk, stride=H), :].astype(BF16),
                                acc_ref, car_ref, heads, u_past)
        return i + 1, lowest < SB_CUT

    done, _ = lax.while_loop(cond, body, (0, True))

    @pl.when(done < n_blocks)
    def _():
        for cp in copies(done):
            cp.wait()

    o_ref[...] = (acc_ref[...] * _silu(z_ref[...].astype(F32))).astype(BF16)


def _stick_breaking(pg3, cache_k, cache_v, layer, H, tq, tk):
    B, T, NC = pg3.shape
    W = H * HEAD_DIM
    blk = lambda g: pl.BlockSpec((None, tq, W), lambda b, i, g=g: (b, i, g))
    scratch = [pltpu.VMEM((tq, W), F32), pltpu.VMEM((H, tq, 1), F32)]
    if cache_k is None:
        kp, vp = pg3, pg3
        kp_spec = pl.BlockSpec((None, T, W), lambda b, i: (b, 0, KA), pipeline_mode=pl.Buffered(1))
        vp_spec = pl.BlockSpec((None, T, W), lambda b, i: (b, 0, VA), pipeline_mode=pl.Buffered(1))
        kern = functools.partial(_sb_kernel, tq=tq, tk=tk, H=H)
    else:
        assert T == tq and cache_k.shape[2] % (tk * H) == 0 and cache_k.shape[2] > 0
        kp, vp = cache_k, cache_v
        kp_spec = vp_spec = pl.BlockSpec(memory_space=pl.ANY)
        kern = functools.partial(_sb_cache_kernel, tq=tq, tk=tk, H=H, layer=layer,
                                 n_blocks=cache_k.shape[2] // (tk * H))
        scratch += [pltpu.VMEM((2, tk * H, HEAD_DIM), F32), pltpu.VMEM((2, tk * H, HEAD_DIM), F32),
                    pltpu.SemaphoreType.DMA((2, 2))]
    return pl.pallas_call(
        kern,
        grid=(B, T // tq),
        in_specs=[blk(QA), blk(KA), blk(VA), blk(ZA), kp_spec, vp_spec],
        out_specs=pl.BlockSpec((None, tq, W), lambda b, i: (b, i, 0)),
        out_shape=jax.ShapeDtypeStruct((B, T, W), BF16),
        scratch_shapes=scratch,
        compiler_params=_cparams(2),
        name="stick_breaking",
    )(pg3, pg3, pg3, pg3, kp, vp)


def _mlstm_kernel(q_ref, k_ref, v_ref, o_ref, z_ref, if_ref, bias_ref, c0_ref, n0_ref, m0_ref,
                  y_ref, c_ref, n_ref, m_ref, *, L, Lc, H):
    ci = pl.program_id(1)
    scale = HEAD_DIM ** -0.5

    @pl.when(ci == 0)
    def _():
        c_ref[...] = c0_ref[...]
        n_ref[...] = n0_ref[...]
        m_ref[...] = m0_ref[...]

    ifv = _pad_rows(if_ref[...] + bias_ref[...], Lc)
    valid = lax.broadcasted_iota(jnp.int32, (Lc, LANES), 0) < L
    ipre = jnp.where(valid, ifv, NEG)
    logf = jnp.where(valid, _log_sigmoid(ifv), 0.0)
    tri = (lax.broadcasted_iota(jnp.int32, (Lc, Lc), 1)
           <= lax.broadcasted_iota(jnp.int32, (Lc, Lc), 0))
    b_all = _split_dot_left(tri.astype(BF16), logf, 3)
    c_all = ipre - pltpu.roll(b_all, LANES - H, axis=1)
    c_all_t = c_all.T

    hs = range(H)
    heads = [slice(h * HEAD_DIM, (h + 1) * HEAD_DIM) for h in hs]
    qh = [_pad_rows(q_ref[:, sl], Lc) for sl in heads]
    kh = [_pad_rows(k_ref[:, sl], Lc) for sl in heads]
    vh = [_pad_rows(v_ref[:, sl], Lc) for sl in heads]
    C = [c_ref[h] for h in hs]
    n_row = [n_ref[h:h + 1, :] for h in hs]
    m_prev = [m_ref[h:h + 1, 0:1] for h in hs]
    s_qk = [_dot_nt(qh[h], kh[h]) * scale for h in hs]
    q_c = [_dot(qh[h], C[h].astype(BF16)) for h in hs]

    bcol = [b_all[:, H + h:H + h + 1] for h in hs]
    b_last = [bcol[h][Lc - 1:Lc, :] for h in hs]
    gcol = [b_last[h] + c_all[:, h:h + 1] for h in hs]
    m_new = [jnp.maximum(m_prev[h] + b_last[h], jnp.max(gcol[h], axis=0, keepdims=True)) for h in hs]
    decay = [jnp.exp(m_prev[h] + b_last[h] - m_new[h]) for h in hs]
    kw = [kh[h].astype(F32) * (scale * jnp.exp(gcol[h] - m_new[h])) for h in hs]
    kw_v = [_dot_tn(kw[h].astype(BF16), vh[h]) for h in hs]
    for h in hs:
        c_ref[h] = decay[h] * C[h] + kw_v[h]
        n_ref[h:h + 1, :] = decay[h] * n_row[h] + jnp.sum(kw[h], axis=0, keepdims=True)
        m_ref[h:h + 1, :] = jnp.broadcast_to(m_new[h], (1, LANES))

    logD = [jnp.where(tri, bcol[h] + c_all_t[h:h + 1, :], NEG) for h in hs]
    inter = [m_prev[h] + bcol[h] for h in hs]
    m_row = [jnp.maximum(inter[h], jnp.max(logD[h], axis=1, keepdims=True)) for h in hs]
    w = [jnp.exp(logD[h] - m_row[h]) * s_qk[h] for h in hs]
    w_v = [_dot(w[h].astype(BF16), vh[h]) for h in hs]
    for h, sl in enumerate(heads):
        a_inter = jnp.exp(inter[h] - m_row[h])
        num = w_v[h] + a_inter * q_c[h]
        qn = jnp.sum(qh[h].astype(F32) * n_row[h], axis=1, keepdims=True)
        den = jnp.sum(w[h], axis=1, keepdims=True) + a_inter * qn
        hh = num / jnp.maximum(jnp.abs(den), jnp.exp(-m_row[h]))
        og = _sigmoid(o_ref[:, sl].astype(F32))
        zg = _silu(z_ref[:, sl].astype(F32))
        y_ref[:, sl] = ((_head_norm(hh)[:L] * og) * zg).astype(BF16)


def _split_dot_left(m, x, terms):
    out = None
    r = x
    for t in range(terms):
        p = r.astype(BF16)
        d = _dot(m, p)
        out = d if out is None else out + d
        if t + 1 < terms:
            r = r - p.astype(F32)
    return out


def _mlstm(pg3, if3, bias_if, C0, n0, m0, L):
    B, T, NC = pg3.shape
    H = C0.shape[1]
    W = H * HEAD_DIM
    Lc = max(L, LANES)
    blk = lambda g: pl.BlockSpec((None, L, W), lambda b, c, g=g: (b, c, g))
    st4 = pl.BlockSpec((None, H, HEAD_DIM, HEAD_DIM), lambda b, c: (b, 0, 0, 0))
    st3 = pl.BlockSpec((None, H, LANES), lambda b, c: (b, 0, 0))
    kern = functools.partial(_mlstm_kernel, L=L, Lc=Lc, H=H)
    return pl.pallas_call(
        kern,
        grid=(B, T // L),
        in_specs=[blk(QB), blk(KB), blk(VB), blk(OB), blk(ZB),
                  pl.BlockSpec((None, L, LANES), lambda b, c: (b, c, 0)),
                  pl.BlockSpec((1, LANES), lambda b, c: (0, 0)),
                  st4, st3, st3],
        out_specs=[pl.BlockSpec((None, L, W), lambda b, c: (b, c, 0)), st4, st3, st3],
        out_shape=[jax.ShapeDtypeStruct((B, T, W), BF16),
                   jax.ShapeDtypeStruct((B, H, HEAD_DIM, HEAD_DIM), F32),
                   jax.ShapeDtypeStruct((B, H, LANES), F32),
                   jax.ShapeDtypeStruct((B, H, LANES), F32)],
        compiler_params=_cparams(2),
        name="mlstm",
    )(pg3, pg3, pg3, pg3, pg3, if3, bias_if, C0, n0, m0)


def _ret_kernel(q_ref, k_ref, v_ref, z_ref, cos_ref, sin_ref, dm_ref, qd_ref, kd_ref, s0_ref,
                y_ref, s_ref, *, L, Lc, H, chunk_decay):
    ci = pl.program_id(1)
    scale = HEAD_DIM ** -0.5

    @pl.when(ci == 0)
    def _():
        s_ref[...] = s0_ref[...]

    cos2 = _pad_rows(cos_ref[...], Lc)
    sin2 = _pad_rows(sin_ref[...], Lc)

    def rot(x):
        return x * cos2 + pltpu.roll(x, HEAD_DIM // 2, axis=1) * sin2

    hs = range(H)
    heads = [slice(h * HEAD_DIM, (h + 1) * HEAD_DIM) for h in hs]
    qb = [rot(_pad_rows(q_ref[:, sl], Lc).astype(F32)).astype(BF16) for sl in heads]
    kr = [rot(_pad_rows(k_ref[:, sl], Lc).astype(F32)) * scale for sl in heads]
    vh = [_pad_rows(v_ref[:, sl], Lc) for sl in heads]
    S = [s_ref[h] for h in hs]
    qk = [_dot_nt(qb[h], kr[h].astype(BF16)) for h in hs]
    q_s = [_dot(qb[h], S[h].astype(BF16)) for h in hs]
    kd_v = [_dot_tn((kr[h] * kd_ref[:, h:h + 1]).astype(BF16), vh[h]) for h in hs]
    for h in hs:
        s_ref[h] = chunk_decay[h] * S[h] + kd_v[h]
    att_v = [_dot((qk[h] * dm_ref[h]).astype(BF16), vh[h]) for h in hs]
    for h, sl in enumerate(heads):
        o = att_v[h] + q_s[h] * qd_ref[:, h:h + 1]
        zg = _silu(z_ref[:, sl].astype(F32))
        y_ref[:, sl] = (_head_norm(o)[:L] * zg).astype(BF16)


def _retention_tables(H, L, Lc):
    lg = np.log1p(-np.exp2(-5.0 - np.arange(H, dtype=np.float64)))
    t = np.arange(Lc, dtype=np.float64)
    rel = t[:, None] - t[None, :]
    dmask = np.where(rel >= 0, np.exp(lg[:, None, None] * np.maximum(rel, 0.0)), 0.0)
    qd = np.zeros((Lc, LANES))
    kd = np.zeros((Lc, LANES))
    qd[:, :H] = np.exp(lg[None, :] * (t[:, None] + 1.0))
    kd[:, :H] = np.where(t[:, None] < L, np.exp(lg[None, :] * (L - 1.0 - t[:, None])), 0.0)
    chunk_decay = tuple(float(v) for v in np.exp(lg * L))
    return (jnp.asarray(dmask, F32), jnp.asarray(qd, F32), jnp.asarray(kd, F32), chunk_decay)


def _retention(pg3, cos2, sin2, S0, L):
    B, T, NC = pg3.shape
    H = S0.shape[1]
    W = H * HEAD_DIM
    Lc = max(L, LANES)
    dmask, qd, kd, chunk_decay = _retention_tables(H, L, Lc)
    blk = lambda g: pl.BlockSpec((None, L, W), lambda b, c, g=g: (b, c, g))
    st4 = pl.BlockSpec((None, H, HEAD_DIM, HEAD_DIM), lambda b, c: (b, 0, 0, 0))
    tab = pl.BlockSpec((L, HEAD_DIM), lambda b, c: (c, 0))
    kern = functools.partial(_ret_kernel, L=L, Lc=Lc, H=H, chunk_decay=chunk_decay)
    return pl.pallas_call(
        kern,
        grid=(B, T // L),
        in_specs=[blk(QC), blk(KC), blk(VC), blk(ZC), tab, tab,
                  pl.BlockSpec((H, Lc, Lc), lambda b, c: (0, 0, 0)),
                  pl.BlockSpec((Lc, LANES), lambda b, c: (0, 0)),
                  pl.BlockSpec((Lc, LANES), lambda b, c: (0, 0)),
                  st4],
        out_specs=[pl.BlockSpec((None, L, W), lambda b, c: (b, c, 0)), st4],
        out_shape=[jax.ShapeDtypeStruct((B, T, W), BF16),
                   jax.ShapeDtypeStruct((B, H, HEAD_DIM, HEAD_DIM), F32)],
        compiler_params=_cparams(2),
        name="retention",
    )(pg3, pg3, pg3, pg3, cos2, sin2, dmask, qd, kd, S0)


def _conv_kernel(u_ref, g_ref, z_ref, c0_ref, w_ref, b_ref, lg_ref, lb_ref, y_ref, c1_ref, xs_ref, cv_ref,
                 sh_ref, *, tt, rows):
    ti = pl.program_id(1)
    hist = CONV_WIDTH - 1
    off = HALO - hist

    @pl.when(ti == 0)
    def _():
        xs_ref[0:HALO, :] = jnp.zeros((HALO, xs_ref.shape[1]), F32)
        xs_ref[off:HALO, :] = c0_ref[...]

    @pl.when(ti > 0)
    def _():
        xs_ref[0:HALO, :] = xs_ref[tt:tt + HALO, :]

    xs_ref[HALO:HALO + tt, :] = u_ref[...].astype(F32) * _sigmoid(g_ref[...].astype(F32))

    for s in range(SUBLANES):
        span = tt + (CONV_WIDTH - 1 - s) // SUBLANES * SUBLANES
        sh_ref[s, 0:span, :] = xs_ref[off + s:off + s + span, :]

    W = xs_ref.shape[1]
    for c in range(0, W, LANES):
        for r in range(0, tt, rows):
            acc = jnp.broadcast_to(b_ref[:, c:c + LANES], (rows, LANES))
            for j in range(CONV_WIDTH):
                a, s = j // SUBLANES * SUBLANES, j % SUBLANES
                acc = acc + w_ref[j:j + 1, c:c + LANES] * sh_ref[s, r + a:r + a + rows, c:c + LANES]
            cv_ref[r:r + rows, c:c + LANES] = acc

    ln = _head_norm(cv_ref[...]) * lg_ref[...] + lb_ref[...]
    y_ref[...] = (_silu(ln) * _silu(z_ref[...].astype(F32))).astype(BF16)
    c1_ref[...] = xs_ref[tt + off:tt + HALO, :]


def _conv(pg3, conv0, conv_w, conv_b, ln_g, ln_b, tt):
    B, T, NC = pg3.shape
    W = conv0.shape[-1]
    hist = CONV_WIDTH - 1
    rows = min(tt, 128)
    blk = lambda g: pl.BlockSpec((None, tt, W), lambda b, t, g=g: (b, t, g))
    vec = pl.BlockSpec((1, W), lambda b, t: (0, 0))
    kern = functools.partial(_conv_kernel, tt=tt, rows=rows)
    return pl.pallas_call(
        kern,
        grid=(B, T // tt),
        in_specs=[blk(UD), blk(GD), blk(ZD),
                  pl.BlockSpec((None, hist, W), lambda b, t: (b, 0, 0)),
                  pl.BlockSpec((CONV_WIDTH, W), lambda b, t: (0, 0)),
                  vec, vec, vec],
        out_specs=[pl.BlockSpec((None, tt, W), lambda b, t: (b, t, 0)),
                   pl.BlockSpec((None, hist, W), lambda b, t: (b, 0, 0))],
        out_shape=[jax.ShapeDtypeStruct((B, T, W), BF16),
                   jax.ShapeDtypeStruct((B, hist, W), F32)],
        scratch_shapes=[pltpu.VMEM((tt + HALO, W), F32), pltpu.VMEM((tt, W), F32),
                        pltpu.VMEM((SUBLANES, tt + HALO - SUBLANES, W), F32)],
        compiler_params=_cparams(2),
        name="conv",
    )(pg3, pg3, pg3, conv0, conv_w, conv_b.reshape(1, W), ln_g.reshape(1, W), ln_b.reshape(1, W))


def _merge_kernel(ya_ref, yb_ref, yc_ref, yd_ref, g_ref, x_ref, wb_ref, wo_ref, fg_ref, o_ref, mg_ref,
                  *, D, W, final):
    ys = (ya_ref, yb_ref, yc_ref, yd_ref)
    for c in range(0, D, W):
        acc = None
        for i in range(N_BRANCH):
            p = _dot(ys[i][...], wb_ref[i, :, c:c + W])
            t = _sigmoid(g_ref[:, i * D + c:i * D + c + W].astype(F32)) * p
            acc = t if acc is None else acc + t
        mg_ref[:, c:c + W] = acc.astype(BF16)
    o = x_ref[...] + _dot(mg_ref[...], wo_ref[...])
    if final:
        ms = jnp.mean(o * o, axis=-1, keepdims=True)
        o = (o * lax.rsqrt(ms + NORM_EPS)) * fg_ref[...]
    o_ref[...] = o


def _merge(ys, pg2, x2, wb, wo, final_g, tm, final):
    N, D = x2.shape
    W = D // N_BRANCH
    yblk = pl.BlockSpec((tm, W), lambda i: (i, 0))
    kern = functools.partial(_merge_kernel, D=D, W=W, final=final)
    return pl.pallas_call(
        kern,
        grid=(N // tm,),
        in_specs=[yblk, yblk, yblk, yblk,
                  pl.BlockSpec((tm, N_BRANCH * D), lambda i: (i, 1)),
                  pl.BlockSpec((tm, D), lambda i: (i, 0)),
                  pl.BlockSpec((N_BRANCH, W, D), lambda i: (0, 0, 0)),
                  pl.BlockSpec((D, D), lambda i: (0, 0)),
                  pl.BlockSpec((1, D), lambda i: (0, 0))],
        out_specs=pl.BlockSpec((tm, D), lambda i: (i, 0)),
        out_shape=jax.ShapeDtypeStruct((N, D), F32),
        scratch_shapes=[pltpu.VMEM((tm, D), BF16)],
        compiler_params=_cparams(1),
        name="merge",
    )(*ys, pg2, x2, wb, wo, final_g.reshape(1, D))


def _rope_tables(pos):
    half = HEAD_DIM // 2
    inv = ROPE_BASE ** (-jnp.arange(half, dtype=F32) / half)
    ang = pos.astype(F32)[:, None] * inv[None, :]
    cos, sin = jnp.cos(ang), jnp.sin(ang)
    return jnp.concatenate([cos, cos], axis=1), jnp.concatenate([-sin, sin], axis=1)


def _wprep_kernel(a_ref, b_ref, g_ref, o_ref, og_ref, *, n_gate, plain_blocks, plain_rows, rows):
    j = pl.program_id(0)
    tn, D = a_ref.shape

    def copy(first_shifted):
        for r in range(0, tn, rows):
            if r < first_shifted:
                v = a_ref[r:r + rows, :]
            elif r + rows + n_gate <= tn:
                v = a_ref[r + n_gate:r + rows + n_gate, :]
            else:
                v = jnp.concatenate([a_ref[r + n_gate:tn, :], b_ref[...]], axis=0)
            o_ref[r:r + rows, :] = v.astype(BF16)

    @pl.when(j < plain_blocks)
    def _():
        copy(tn)

    @pl.when(j == plain_blocks)
    def _():
        copy(plain_rows)

    @pl.when(j > plain_blocks)
    def _():
        copy(0)

    @pl.when(j == 0)
    def _():
        og_ref[...] = jnp.concatenate([g_ref[...], jnp.zeros((LANES - n_gate, D), F32)], axis=0).astype(BF16)


def _wprep(wt_in, layer, H):
    _, n_in, D = wt_in.shape
    W = D // N_BRANCH
    n_gate = 2 * H
    assert n_gate == SUBLANES
    s = (ZB + 1) * W
    NC = n_in - n_gate
    tn = 2 * W
    rows = 256
    assert (s % tn) % rows == 0
    kern = functools.partial(_wprep_kernel, n_gate=n_gate, plain_blocks=s // tn, plain_rows=s % tn, rows=rows)
    return pl.pallas_call(
        kern,
        grid=(NC // tn,),
        in_specs=[pl.BlockSpec((None, tn, D), lambda j: (layer, j, 0)),
                  pl.BlockSpec((None, n_gate, D), lambda j: (layer, (j + 1) * (tn // n_gate), 0)),
                  pl.BlockSpec((None, n_gate, D), lambda j: (layer, s // n_gate, 0))],
        out_specs=[pl.BlockSpec((tn, D), lambda j: (j, 0)),
                   pl.BlockSpec((LANES, D), lambda j: (0, 0))],
        out_shape=[jax.ShapeDtypeStruct((NC, D), BF16), jax.ShapeDtypeStruct((LANES, D), BF16)],
        compiler_params=_cparams(1),
        name="wprep",
    )(wt_in, wt_in, wt_in)


def _prep_layer(wt_in, layer, b_i_l, b_f_l, w_branch_l, w_out_l, H):
    w_main, w_if = _wprep(wt_in, layer, H)
    bias_if = jnp.pad(jnp.concatenate([b_i_l, b_f_l]).astype(F32), (0, LANES - 2 * H)).reshape(1, LANES)
    return w_main, w_if, bias_if, w_branch_l.astype(BF16), w_out_l.astype(BF16)


def _trunk(x, past_len, cache_k, cache_v, C0, n0, m0, S0, conv0, layers, conv_w, conv_b, ln_g, ln_b,
           norm_g, final_g, cfg):
    B, T, D = x.shape
    N = B * T
    W = D // N_BRANCH
    H = W // HEAD_DIM
    depth = len(layers)
    cos2, sin2 = _rope_tables(past_len + jnp.arange(T))
    x2 = x.reshape(N, D)
    if cache_k is None:
        ck4 = cv4 = None
    else:
        ck4 = cache_k.reshape(depth, B, -1, HEAD_DIM)
        cv4 = cache_v.reshape(depth, B, -1, HEAD_DIM)
    states = []
    kv = None
    for l in range(depth):
        w_main, w_if, bias_if, wb, wo = layers[l]
        pg, k_all, v_all, ifp = _inproj(x2, norm_g[l], w_main, w_if, kv, l, depth, H, cfg["tm_in"])
        kv = (k_all, v_all)
        pg3 = pg.reshape(B, T, -1)
        ya = _stick_breaking(pg3, ck4, cv4, l, H, cfg["tq"], cfg["tk"])
        m0b = jnp.broadcast_to(m0[l].astype(F32)[..., None], (B, H, LANES))
        yb, C1, n1, m1 = _mlstm(pg3, ifp.reshape(B, T, LANES), bias_if, C0[l].astype(F32),
                                n0[l].astype(F32), m0b, cfg["L"])
        yc, S1 = _retention(pg3, cos2, sin2, S0[l].astype(F32), cfg["L"])
        yd, conv1 = _conv(pg3, conv0[l], conv_w[l], conv_b[l], ln_g[l], ln_b[l], cfg["tt"])
        ys = tuple(y.reshape(N, W) for y in (ya, yb, yc, yd))
        x2 = _merge(ys, pg, x2, wb, wo, final_g, cfg["tm_mg"], final=(l == depth - 1))
        states.append((C1, n1, m1[..., 0], S1, conv1))
    stacked = tuple(jnp.stack(s, axis=0) for s in zip(*states))
    kv5 = tuple(a.reshape(depth, B, T, H, HEAD_DIM) for a in kv)
    return x2.reshape(B, T, D), kv5 + stacked


def _config(T):
    if T >= 256:
        return dict(tm_in=1024, tq=256, tk=256, L=256, tt=256, tm_mg=256)
    return dict(tm_in=512, tq=T, tk=128, L=T, tt=T, tm_mg=256)


def kernel(x_prompt, x_sample, cache_sb_k, cache_sb_v, state_mlstm_C, state_mlstm_n, state_mlstm_m,
           state_ret_S, state_conv, norm_g, w_in, mlstm_b_i, mlstm_b_f, conv_w, conv_b, conv_ln_g,
           conv_ln_b, w_branch, w_out, final_g):
    depth = w_in.shape[0]
    Bp, Tp, D = x_prompt.shape
    W = D // N_BRANCH
    H = W // HEAD_DIM
    wt_in = jnp.swapaxes(w_in, 1, 2)
    layers = [_prep_layer(wt_in, l, mlstm_b_i[l], mlstm_b_f[l], w_branch[l], w_out[l], H) for l in range(depth)]
    zC = jnp.zeros((depth, Bp, H, HEAD_DIM, HEAD_DIM), F32)
    zn = jnp.zeros((depth, Bp, H, HEAD_DIM), F32)
    zm = jnp.zeros((depth, Bp, H), F32)
    zconv = jnp.zeros((depth, Bp, CONV_WIDTH - 1, W), x_prompt.dtype)
    shared = (layers, conv_w, conv_b, conv_ln_g, conv_ln_b, norm_g, final_g)
    y_p, (pk, pv, pC, pn, pm, pS, pconv) = _trunk(
        x_prompt, 0, None, None, zC, zn, zm, zC, zconv, *shared, _config(Tp))
    y_s, (sk, sv, sC, sn, sm, sS, sconv) = _trunk(
        x_sample, cache_sb_k.shape[2], cache_sb_k, cache_sb_v, state_mlstm_C, state_mlstm_n,
        state_mlstm_m, state_ret_S, state_conv, *shared, _config(x_sample.shape[1]))
    return (y_p, y_s, pk, pv, pC, pn, pm, pS, pconv, sk, sv, sC, sn, sm, sS, sconv)
```

```python
import functools

import numpy as np
import jax
import jax.numpy as jnp
from jax import lax
from jax.experimental import pallas as pl
from jax.experimental.pallas import tpu as pltpu

F32 = jnp.float32
BF16 = jnp.bfloat16

HEAD_DIM = 128
N_BRANCH = 4
CONV_WIDTH = 31
ROPE_BASE = 10000.0
NORM_EPS = 1e-6
LN_EPS = 1e-5
LANES = 128
SUBLANES = 8
HALO = 32
NEG = -1e30
SB_CUT = 104.0
VMEM_LIMIT = 56 * 1024 * 1024

QA, KA, VA, ZA, QB, KB, VB, OB, ZB, QC, KC, VC, ZC, UD, GD, ZD = range(16)


def _cparams(n_axes):
    return pltpu.CompilerParams(dimension_semantics=("arbitrary",) * n_axes,
                                vmem_limit_bytes=VMEM_LIMIT)


def _dot(a, b):
    return jnp.dot(a, b, preferred_element_type=F32)


def _dot_nt(a, b):
    return lax.dot_general(a, b, (((1,), (1,)), ((), ())), preferred_element_type=F32)


def _dot_tn(a, b):
    return lax.dot_general(a, b, (((0,), (0,)), ((), ())), preferred_element_type=F32)


def _softplus(x):
    return jnp.maximum(x, 0.0) + jnp.log1p(jnp.exp(-jnp.abs(x)))


def _softplus_plain_log(x):
    return jnp.maximum(x, 0.0) + jnp.log(1.0 + jnp.exp(-jnp.abs(x)))


def _log_sigmoid(x):
    return -_softplus(-x)


def _sigmoid(x):
    return 1.0 / (1.0 + jnp.exp(-x))


def _silu(x):
    return x * _sigmoid(x)


def _head_norm(x):
    mu = jnp.mean(x, axis=-1, keepdims=True)
    xc = x - mu
    return xc * lax.rsqrt(jnp.mean(xc * xc, axis=-1, keepdims=True) + LN_EPS)


def _head_norm_mxu(x):
    j = jnp.full((HEAD_DIM, HEAD_DIM), 1.0 / HEAD_DIM, BF16)
    xc = x - _split_dot(x, j, 2)
    return xc * lax.rsqrt(_split_dot(xc * xc, j, 2) + LN_EPS)


def _run_staged(stage_iters, lead):
    live = list(stage_iters)
    for it, n in zip(live, lead):
        for _ in range(n):
            next(it, None)
    while live:
        for it in list(live):
            if next(it, StopIteration) is StopIteration:
                live.remove(it)


def _split_dot(x, m, terms):
    out = None
    r = x
    for t in range(terms):
        p = r.astype(BF16)
        d = _dot(p, m)
        out = d if out is None else out + d
        if t + 1 < terms:
            r = r - p.astype(F32)
    return out


def _pad_rows(x, rows):
    if x.shape[0] == rows:
        return x
    return jnp.concatenate([x, jnp.zeros((rows - x.shape[0],) + x.shape[1:], x.dtype)], axis=0)


def _inproj_kernel(x_ref, g_ref, w_ref, wif_ref, kprev_ref, vprev_ref, pg_ref, k_ref, v_ref, if_ref, hin_ref,
                   *, W, H, rows):
    del kprev_ref, vprev_ref
    j = pl.program_id(1)
    tm = x_ref.shape[0]

    @pl.when(j == 0)
    def _():
        for r in range(0, tm, rows):
            x = x_ref[r:r + rows, :]
            ms = jnp.mean(x * x, axis=-1, keepdims=True)
            hb = ((x * lax.rsqrt(ms + NORM_EPS)) * g_ref[...]).astype(BF16)
            hin_ref[r:r + rows, :] = hb
            if_ref[r:r + rows, :] = _dot_nt(hb, wif_ref[...])

    acc = _dot_nt(hin_ref[...], w_ref[...])
    pg_ref[...] = acc.astype(BF16)

    def put_state(ref, col0):
        for h in range(H):
            ref[pl.ds(h, tm, stride=H), :] = acc[:, col0 + h * HEAD_DIM:col0 + (h + 1) * HEAD_DIM]

    @pl.when(j == 0)
    def _():
        put_state(k_ref, KA * W)

    @pl.when(j == 1)
    def _():
        put_state(v_ref, (VA - 2) * W)


def _inproj(x2, norm_g, wt_main, wt_if, kv_prev, layer, depth, H, tm):
    N, D = x2.shape
    NC = wt_main.shape[0]
    W = D // N_BRANCH
    tn = 2 * W
    rows = min(tm, 256)
    kern = functools.partial(_inproj_kernel, W=W, H=H, rows=rows)
    kv_spec = pl.BlockSpec((None, tm * H, HEAD_DIM), lambda i, j: (layer, i, 0))
    kv_shape = jax.ShapeDtypeStruct((depth, N * H, HEAD_DIM), F32)
    if kv_prev is None:
        kv_prev = (jnp.zeros(kv_shape.shape, F32), jnp.zeros(kv_shape.shape, F32))
    return pl.pallas_call(
        kern,
        grid=(N // tm, NC // tn),
        in_specs=[
            pl.BlockSpec((tm, D), lambda i, j: (i, 0)),
            pl.BlockSpec((1, D), lambda i, j: (0, 0)),
            pl.BlockSpec((tn, D), lambda i, j: (j, 0)),
            pl.BlockSpec((LANES, D), lambda i, j: (0, 0)),
            pl.BlockSpec(memory_space=pl.ANY),
            pl.BlockSpec(memory_space=pl.ANY),
        ],
        out_specs=[
            pl.BlockSpec((tm, tn), lambda i, j: (i, j)),
            kv_spec,
            kv_spec,
            pl.BlockSpec((tm, LANES), lambda i, j: (i, 0)),
        ],
        out_shape=[jax.ShapeDtypeStruct((N, NC), BF16), kv_shape, kv_shape,
                   jax.ShapeDtypeStruct((N, LANES), F32)],
        scratch_shapes=[pltpu.VMEM((tm, D), BF16)],
        input_output_aliases={4: 1, 5: 2},
        compiler_params=_cparams(2),
        name="inproj",
    )(x2, norm_g.reshape(1, D), wt_main, wt_if, *kv_prev)


SB_SCALE = HEAD_DIM ** -0.5


def _strict_upper(n):
    return (lax.broadcasted_iota(jnp.int32, (n, n), 0)
            > lax.broadcasted_iota(jnp.int32, (n, n), 1)).astype(BF16)


def _sb_new_rows(q_ref, kn_ref, vn_ref, acc_ref, car_ref, heads, tq):
    tkd = max(tq, LANES)
    u_new = _strict_upper(tkd)
    causal = (lax.broadcasted_iota(jnp.int32, (tq, tkd), 1)
              < lax.broadcasted_iota(jnp.int32, (tq, tkd), 0))
    zs = [_dot_nt(q_ref[:, sl], _pad_rows(kn_ref[:, sl], tkd)) * SB_SCALE for sl in heads]
    sps = [jnp.where(causal, _softplus_plain_log(z), 0.0) for z in zs]
    cums = [_split_dot(sp, u_new, 2) for sp in sps]
    ws = [jnp.where(causal, jnp.exp(z - sp - cum), 0.0).astype(BF16) for z, sp, cum in zip(zs, sps, cums)]
    for h, sl in enumerate(heads):
        acc_ref[:, sl] = _dot(ws[h], _pad_rows(vn_ref[:, sl], tkd))
        car_ref[h] = jnp.sum(sps[h], axis=1, keepdims=True)


def _sb_past_block(q_ref, k_of, v_of, acc_ref, car_ref, heads, u_past):
    zb = [_dot_nt(q_ref[:, sl], k_of(h)) * SB_SCALE for h, sl in enumerate(heads)]
    spb = [_softplus_plain_log(z) for z in zb]
    cumb = [_split_dot(sp, u_past, 2) for sp in spb]
    cars = [car_ref[h] for h in range(len(heads))]
    wb = [jnp.exp(z - sp - cum - car).astype(BF16) for z, sp, cum, car in zip(zb, spb, cumb, cars)]
    lowest = None
    for h, sl in enumerate(heads):
        acc_ref[:, sl] += _dot(wb[h], v_of(h))
        car_new = cars[h] + jnp.sum(spb[h], axis=1, keepdims=True)
        car_ref[h] = car_new
        lowest = car_new if lowest is None else jnp.minimum(lowest, car_new)
    return jnp.min(lowest)


def _sb_kernel(q_ref, kn_ref, vn_ref, z_ref, kp_ref, vp_ref, o_ref, acc_ref, car_ref, *, tq, tk, H):
    qi = pl.program_id(1)
    heads = [slice(h * HEAD_DIM, (h + 1) * HEAD_DIM) for h in range(H)]
    _sb_new_rows(q_ref, kn_ref, vn_ref, acc_ref, car_ref, heads, tq)
    u_past = _strict_upper(tk)

    def cond(c):
        kb, go = c
        return jnp.logical_and(kb >= 0, go)

    def body(c):
        kb, _ = c
        start = pl.multiple_of(kb * tk, tk)
        lowest = _sb_past_block(q_ref, lambda h: kp_ref[pl.ds(start, tk), heads[h]],
                                lambda h: vp_ref[pl.ds(start, tk), heads[h]], acc_ref, car_ref, heads, u_past)
        return kb - 1, lowest < SB_CUT

    lax.while_loop(cond, body, ((qi * tq) // tk - 1, True))
    o_ref[...] = (acc_ref[...] * _silu(z_ref[...].astype(F32))).astype(BF16)


def _sb_cache_kernel(q_ref, kn_ref, vn_ref, z_ref, kc_ref, vc_ref, o_ref, acc_ref, car_ref, kbuf, vbuf, sem,
                     *, tq, tk, H, layer, n_blocks):
    b = pl.program_id(0)
    heads = [slice(h * HEAD_DIM, (h + 1) * HEAD_DIM) for h in range(H)]
    rows = tk * H

    def copies(i):
        src = pl.ds((n_blocks - 1 - i) * rows, rows)
        slot = i % 2
        return (pltpu.make_async_copy(kc_ref.at[layer, b, src, :], kbuf.at[slot], sem.at[0, slot]),
                pltpu.make_async_copy(vc_ref.at[layer, b, src, :], vbuf.at[slot], sem.at[1, slot]))

    for c in copies(0):
        c.start()
    _sb_new_rows(q_ref, kn_ref, vn_ref, acc_ref, car_ref, heads, tq)
    u_past = _strict_upper(tk)

    def cond(c):
        i, go = c
        return jnp.logical_and(i < n_blocks, go)

    def body(c):
        i, _ = c
        slot = i % 2
        for cp in copies(i):
            cp.wait()

        @pl.when(i + 1 < n_blocks)
        def _():
            for cp in copies(i + 1):
                cp.start()

        lowest = _sb_past_block(q_ref, lambda h: kbuf[slot, pl.ds(h, tk, stride=H), :].astype(BF16),
                                lambda h: vbuf[slot, pl.ds(h, tk, stride=H), :].astype(BF16),
                                acc_ref, car_ref, heads, u_past)
        return i + 1, lowest < SB_CUT

    done, _ = lax.while_loop(cond, body, (0, True))

    @pl.when(done < n_blocks)
    def _():
        for cp in copies(done):
            cp.wait()

    o_ref[...] = (acc_ref[...] * _silu(z_ref[...].astype(F32))).astype(BF16)


def _stick_breaking(pg3, cache_k, cache_v, layer, H, tq, tk):
    B, T, NC = pg3.shape
    W = H * HEAD_DIM
    blk = lambda g: pl.BlockSpec((None, tq, W), lambda b, i, g=g: (b, i, g))
    scratch = [pltpu.VMEM((tq, W), F32), pltpu.VMEM((H, tq, 1), F32)]
    if cache_k is None:
        kp, vp = pg3, pg3
        kp_spec = pl.BlockSpec((None, T, W), lambda b, i: (b, 0, KA), pipeline_mode=pl.Buffered(1))
        vp_spec = pl.BlockSpec((None, T, W), lambda b, i: (b, 0, VA), pipeline_mode=pl.Buffered(1))
        kern = functools.partial(_sb_kernel, tq=tq, tk=tk, H=H)
    else:
        assert T == tq and cache_k.shape[2] % (tk * H) == 0 and cache_k.shape[2] > 0
        kp, vp = cache_k, cache_v
        kp_spec = vp_spec = pl.BlockSpec(memory_space=pl.ANY)
        kern = functools.partial(_sb_cache_kernel, tq=tq, tk=tk, H=H, layer=layer,
                                 n_blocks=cache_k.shape[2] // (tk * H))
        scratch += [pltpu.VMEM((2, tk * H, HEAD_DIM), F32), pltpu.VMEM((2, tk * H, HEAD_DIM), F32),
                    pltpu.SemaphoreType.DMA((2, 2))]
    return pl.pallas_call(
        kern,
        grid=(B, T // tq),
        in_specs=[blk(QA), blk(KA), blk(VA), blk(ZA), kp_spec, vp_spec],
        out_specs=pl.BlockSpec((None, tq, W), lambda b, i: (b, i, 0)),
        out_shape=jax.ShapeDtypeStruct((B, T, W), BF16),
        scratch_shapes=scratch,
        compiler_params=_cparams(2),
        name="stick_breaking",
    )(pg3, pg3, pg3, pg3, kp, vp)


def _mlstm_stages(q_ref, k_ref, v_ref, o_ref, z_ref, if_ref, bias_ref, c0_ref, n0_ref, m0_ref,
                  y_ref, c_ref, n_ref, m_ref, *, L, Lc, H):
    ci = pl.program_id(1)
    scale = HEAD_DIM ** -0.5

    @pl.when(ci == 0)
    def _():
        c_ref[...] = c0_ref[...]
        n_ref[...] = n0_ref[...]
        m_ref[...] = m0_ref[...]

    yield
    ifv = _pad_rows(if_ref[...] + bias_ref[...], Lc)
    valid = lax.broadcasted_iota(jnp.int32, (Lc, LANES), 0) < L
    ipre = jnp.where(valid, ifv, NEG)
    logf = jnp.where(valid, _log_sigmoid(ifv), 0.0)
    tri = (lax.broadcasted_iota(jnp.int32, (Lc, Lc), 1)
           <= lax.broadcasted_iota(jnp.int32, (Lc, Lc), 0))
    b_all = _split_dot_left(tri.astype(BF16), logf, 3)
    c_all = ipre - pltpu.roll(b_all, LANES - H, axis=1)
    c_all_t = c_all.T

    hs = range(H)
    heads = [slice(h * HEAD_DIM, (h + 1) * HEAD_DIM) for h in hs]
    qh = [_pad_rows(q_ref[:, sl], Lc) for sl in heads]
    kh = [_pad_rows(k_ref[:, sl], Lc) for sl in heads]
    vh = [_pad_rows(v_ref[:, sl], Lc) for sl in heads]
    C = [c_ref[h] for h in hs]
    n_row = [n_ref[h:h + 1, :] for h in hs]
    m_prev = [m_ref[h:h + 1, 0:1] for h in hs]
    s_qk = [_dot_nt(qh[h], kh[h]) * scale for h in hs]
    q_c = [_dot(qh[h], C[h].astype(BF16)) for h in hs]

    yield
    bcol = [b_all[:, H + h:H + h + 1] for h in hs]
    b_last = [bcol[h][Lc - 1:Lc, :] for h in hs]
    gcol = [b_last[h] + c_all[:, h:h + 1] for h in hs]
    m_new = [jnp.maximum(m_prev[h] + b_last[h], jnp.max(gcol[h], axis=0, keepdims=True)) for h in hs]
    decay = [jnp.exp(m_prev[h] + b_last[h] - m_new[h]) for h in hs]
    kw = [kh[h].astype(F32) * (scale * jnp.exp(gcol[h] - m_new[h])) for h in hs]
    kw_v = [_dot_tn(kw[h].astype(BF16), vh[h]) for h in hs]
    for h in hs:
        c_ref[h] = decay[h] * C[h] + kw_v[h]
        n_ref[h:h + 1, :] = decay[h] * n_row[h] + jnp.sum(kw[h], axis=0, keepdims=True)
        m_ref[h:h + 1, :] = jnp.broadcast_to(m_new[h], (1, LANES))

    yield
    inter = [m_prev[h] + bcol[h] for h in hs]
    m_row, w, w_v = [], [], []
    for h in hs:
        logD = jnp.where(tri, bcol[h] + c_all_t[h:h + 1, :], NEG)
        m_row.append(jnp.maximum(inter[h], jnp.max(logD, axis=1, keepdims=True)))
        w.append(jnp.exp(logD - m_row[h]) * s_qk[h])
        w_v.append(_dot(w[h].astype(BF16), vh[h]))
        yield
    hh = []
    for h in hs:
        a_inter = jnp.exp(inter[h] - m_row[h])
        num = w_v[h] + a_inter * q_c[h]
        qn = jnp.sum(qh[h].astype(F32) * n_row[h], axis=1, keepdims=True)
        den = jnp.sum(w[h], axis=1, keepdims=True) + a_inter * qn
        hh.append(num / jnp.maximum(jnp.abs(den), jnp.exp(-m_row[h])))
        yield
    hn = [_head_norm_mxu(x) for x in hh]
    for h, sl in enumerate(heads):
        og = _sigmoid(o_ref[:, sl].astype(F32))
        zg = _silu(z_ref[:, sl].astype(F32))
        y_ref[:, sl] = ((hn[h][:L] * og) * zg).astype(BF16)
        yield


def _split_dot_left(m, x, terms):
    out = None
    r = x
    for t in range(terms):
        p = r.astype(BF16)
        d = _dot(m, p)
        out = d if out is None else out + d
        if t + 1 < terms:
            r = r - p.astype(F32)
    return out


class _Branch:
    def __init__(self, args, in_specs, out_specs, out_shapes, scratch, stages, lead=0):
        self.args, self.in_specs, self.out_specs = args, in_specs, out_specs
        self.out_shapes, self.scratch, self.stages = out_shapes, scratch, stages
        self.lead = lead


def _mlstm_branch(pg3, if3, bias_if, C0, n0, m0, L):
    B, T, NC = pg3.shape
    H = C0.shape[1]
    W = H * HEAD_DIM
    Lc = max(L, LANES)
    blk = lambda g: pl.BlockSpec((None, L, W), lambda b, c, g=g: (b, c, g))
    st4 = pl.BlockSpec((None, H, HEAD_DIM, HEAD_DIM), lambda b, c: (b, 0, 0, 0))
    st3 = pl.BlockSpec((None, H, LANES), lambda b, c: (b, 0, 0))
    return _Branch(
        args=[pg3, pg3, pg3, pg3, pg3, if3, bias_if, C0, n0, m0],
        in_specs=[blk(QB), blk(KB), blk(VB), blk(OB), blk(ZB),
                  pl.BlockSpec((None, L, LANES), lambda b, c: (b, c, 0)),
                  pl.BlockSpec((1, LANES), lambda b, c: (0, 0)),
                  st4, st3, st3],
        out_specs=[pl.BlockSpec((None, L, W), lambda b, c: (b, c, 0)), st4, st3, st3],
        out_shapes=[jax.ShapeDtypeStruct((B, T, W), BF16),
                    jax.ShapeDtypeStruct((B, H, HEAD_DIM, HEAD_DIM), F32),
                    jax.ShapeDtypeStruct((B, H, LANES), F32),
                    jax.ShapeDtypeStruct((B, H, LANES), F32)],
        scratch=[],
        stages=functools.partial(_mlstm_stages, L=L, Lc=Lc, H=H))


def _ret_stages(q_ref, k_ref, v_ref, z_ref, cos_ref, sin_ref, dm_ref, qd_ref, kd_ref, s0_ref,
                y_ref, s_ref, *, L, Lc, H, chunk_decay):
    ci = pl.program_id(1)
    scale = HEAD_DIM ** -0.5

    @pl.when(ci == 0)
    def _():
        s_ref[...] = s0_ref[...]

    yield
    cos2 = _pad_rows(cos_ref[...], Lc)
    sin2 = _pad_rows(sin_ref[...], Lc)

    def rot(x):
        return x * cos2 + pltpu.roll(x, HEAD_DIM // 2, axis=1) * sin2

    hs = range(H)
    heads = [slice(h * HEAD_DIM, (h + 1) * HEAD_DIM) for h in hs]
    qb = [rot(_pad_rows(q_ref[:, sl], Lc).astype(F32)).astype(BF16) for sl in heads]
    kr = [rot(_pad_rows(k_ref[:, sl], Lc).astype(F32)) * scale for sl in heads]
    vh = [_pad_rows(v_ref[:, sl], Lc) for sl in heads]
    S = [s_ref[h] for h in hs]
    qk = [_dot_nt(qb[h], kr[h].astype(BF16)) for h in hs]
    q_s = [_dot(qb[h], S[h].astype(BF16)) for h in hs]
    yield
    kd_v = [_dot_tn((kr[h] * kd_ref[:, h:h + 1]).astype(BF16), vh[h]) for h in hs]
    for h in hs:
        s_ref[h] = chunk_decay[h] * S[h] + kd_v[h]
    yield
    att_v = [_dot((qk[h] * dm_ref[h]).astype(BF16), vh[h]) for h in hs]
    yield
    hn = [_head_norm_mxu(att_v[h] + q_s[h] * qd_ref[:, h:h + 1]) for h in hs]
    for h, sl in enumerate(heads):
        zg = _silu(z_ref[:, sl].astype(F32))
        y_ref[:, sl] = (hn[h][:L] * zg).astype(BF16)


def _retention_tables(H, L, Lc):
    lg = np.log1p(-np.exp2(-5.0 - np.arange(H, dtype=np.float64)))
    t = np.arange(Lc, dtype=np.float64)
    rel = t[:, None] - t[None, :]
    dmask = np.where(rel >= 0, np.exp(lg[:, None, None] * np.maximum(rel, 0.0)), 0.0)
    qd = np.zeros((Lc, LANES))
    kd = np.zeros((Lc, LANES))
    qd[:, :H] = np.exp(lg[None, :] * (t[:, None] + 1.0))
    kd[:, :H] = np.where(t[:, None] < L, np.exp(lg[None, :] * (L - 1.0 - t[:, None])), 0.0)
    chunk_decay = tuple(float(v) for v in np.exp(lg * L))
    return (jnp.asarray(dmask, F32), jnp.asarray(qd, F32), jnp.asarray(kd, F32), chunk_decay)


def _retention_branch(pg3, cos2, sin2, S0, L):
    B, T, NC = pg3.shape
    H = S0.shape[1]
    W = H * HEAD_DIM
    Lc = max(L, LANES)
    dmask, qd, kd, chunk_decay = _retention_tables(H, L, Lc)
    blk = lambda g: pl.BlockSpec((None, L, W), lambda b, c, g=g: (b, c, g))
    st4 = pl.BlockSpec((None, H, HEAD_DIM, HEAD_DIM), lambda b, c: (b, 0, 0, 0))
    tab = pl.BlockSpec((L, HEAD_DIM), lambda b, c: (c, 0))
    return _Branch(
        args=[pg3, pg3, pg3, pg3, cos2, sin2, dmask, qd, kd, S0],
        in_specs=[blk(QC), blk(KC), blk(VC), blk(ZC), tab, tab,
                  pl.BlockSpec((H, Lc, Lc), lambda b, c: (0, 0, 0)),
                  pl.BlockSpec((Lc, LANES), lambda b, c: (0, 0)),
                  pl.BlockSpec((Lc, LANES), lambda b, c: (0, 0)),
                  st4],
        out_specs=[pl.BlockSpec((None, L, W), lambda b, c: (b, c, 0)), st4],
        out_shapes=[jax.ShapeDtypeStruct((B, T, W), BF16),
                    jax.ShapeDtypeStruct((B, H, HEAD_DIM, HEAD_DIM), F32)],
        scratch=[],
        stages=functools.partial(_ret_stages, L=L, Lc=Lc, H=H, chunk_decay=chunk_decay))


def _conv_stages(u_ref, g_ref, z_ref, c0_ref, w_ref, b_ref, lg_ref, lb_ref, y_ref, c1_ref, xs_ref, cv_ref,
                 sh_ref, *, tt, rows):
    ti = pl.program_id(1)
    hist = CONV_WIDTH - 1
    off = HALO - hist
    W = xs_ref.shape[1]

    @pl.when(ti == 0)
    def _():
        xs_ref[0:HALO, :] = jnp.zeros((HALO, W), F32)
        xs_ref[off:HALO, :] = c0_ref[...]

    @pl.when(ti > 0)
    def _():
        xs_ref[0:HALO, :] = xs_ref[tt:tt + HALO, :]

    yield
    for r in range(0, tt, rows):
        xs_ref[HALO + r:HALO + r + rows, :] = (u_ref[r:r + rows, :].astype(F32)
                                               * _sigmoid(g_ref[r:r + rows, :].astype(F32)))
    c1_ref[...] = xs_ref[tt + off:tt + HALO, :]

    for s in range(SUBLANES):
        span = tt + (CONV_WIDTH - 1 - s) // SUBLANES * SUBLANES
        sh_ref[s, 0:span, :] = xs_ref[off + s:off + s + span, :]

    yield
    for r in range(0, tt, rows):
        for c in range(0, W, LANES):
            acc = jnp.broadcast_to(b_ref[:, c:c + LANES], (rows, LANES))
            for j in range(CONV_WIDTH):
                a, s = j // SUBLANES * SUBLANES, j % SUBLANES
                acc = acc + w_ref[j:j + 1, c:c + LANES] * sh_ref[s, r + a:r + a + rows, c:c + LANES]
            cv_ref[r:r + rows, c:c + LANES] = acc
            yield
        ln = _head_norm(cv_ref[r:r + rows, :]) * lg_ref[...] + lb_ref[...]
        y_ref[r:r + rows, :] = (_silu(ln) * _silu(z_ref[r:r + rows, :].astype(F32))).astype(BF16)
        yield


def _conv_branch(pg3, conv0, conv_w, conv_b, ln_g, ln_b, tt):
    B, T, NC = pg3.shape
    W = conv0.shape[-1]
    hist = CONV_WIDTH - 1
    rows = min(tt, 64)
    blk = lambda g: pl.BlockSpec((None, tt, W), lambda b, t, g=g: (b, t, g))
    vec = pl.BlockSpec((1, W), lambda b, t: (0, 0))
    return _Branch(
        args=[pg3, pg3, pg3, conv0, conv_w, conv_b.reshape(1, W), ln_g.reshape(1, W), ln_b.reshape(1, W)],
        in_specs=[blk(UD), blk(GD), blk(ZD),
                  pl.BlockSpec((None, hist, W), lambda b, t: (b, 0, 0)),
                  pl.BlockSpec((CONV_WIDTH, W), lambda b, t: (0, 0)),
                  vec, vec, vec],
        out_specs=[pl.BlockSpec((None, tt, W), lambda b, t: (b, t, 0)),
                   pl.BlockSpec((None, hist, W), lambda b, t: (b, 0, 0))],
        out_shapes=[jax.ShapeDtypeStruct((B, T, W), BF16),
                    jax.ShapeDtypeStruct((B, hist, W), F32)],
        scratch=[pltpu.VMEM((tt + HALO, W), F32), pltpu.VMEM((tt, W), F32),
                 pltpu.VMEM((SUBLANES, tt + HALO - SUBLANES, W), F32)],
        stages=functools.partial(_conv_stages, tt=tt, rows=rows),
        lead=2)


def _mixers_kernel(*refs, branches):
    n_in = sum(len(b.in_specs) for b in branches)
    n_out = sum(len(b.out_specs) for b in branches)
    ins, outs, scr = list(refs[:n_in]), list(refs[n_in:n_in + n_out]), list(refs[n_in + n_out:])
    iters = []
    for b in branches:
        take = lambda pool, n: [pool.pop(0) for _ in range(n)]
        iters.append(b.stages(*take(ins, len(b.in_specs)), *take(outs, len(b.out_specs)),
                              *take(scr, len(b.scratch))))
    _run_staged(iters, [b.lead for b in branches])


def _mixers(branches, grid):
    outs = pl.pallas_call(
        functools.partial(_mixers_kernel, branches=branches),
        grid=grid,
        in_specs=[s for b in branches for s in b.in_specs],
        out_specs=[s for b in branches for s in b.out_specs],
        out_shape=[s for b in branches for s in b.out_shapes],
        scratch_shapes=[s for b in branches for s in b.scratch],
        compiler_params=_cparams(len(grid)),
        name="mixers",
    )(*[a for b in branches for a in b.args])
    split, k = [], 0
    for b in branches:
        split.append(outs[k:k + len(b.out_specs)])
        k += len(b.out_specs)
    return split


def _merge_kernel(ya_ref, yb_ref, yc_ref, yd_ref, g_ref, x_ref, wb_ref, wo_ref, fg_ref, o_ref, mg_ref,
                  *, D, W, final):
    ys = (ya_ref, yb_ref, yc_ref, yd_ref)
    for c in range(0, D, W):
        acc = None
        for i in range(N_BRANCH):
            p = _dot(ys[i][...], wb_ref[i, :, c:c + W])
            t = _sigmoid(g_ref[:, i * D + c:i * D + c + W].astype(F32)) * p
            acc = t if acc is None else acc + t
        mg_ref[:, c:c + W] = acc.astype(BF16)
    o = x_ref[...] + _dot(mg_ref[...], wo_ref[...])
    if final:
        ms = jnp.mean(o * o, axis=-1, keepdims=True)
        o = (o * lax.rsqrt(ms + NORM_EPS)) * fg_ref[...]
    o_ref[...] = o


def _merge(ys, pg2, x2, wb, wo, final_g, tm, final):
    N, D = x2.shape
    W = D // N_BRANCH
    yblk = pl.BlockSpec((tm, W), lambda i: (i, 0))
    kern = functools.partial(_merge_kernel, D=D, W=W, final=final)
    return pl.pallas_call(
        kern,
        grid=(N // tm,),
        in_specs=[yblk, yblk, yblk, yblk,
                  pl.BlockSpec((tm, N_BRANCH * D), lambda i: (i, 1)),
                  pl.BlockSpec((tm, D), lambda i: (i, 0)),
                  pl.BlockSpec((N_BRANCH, W, D), lambda i: (0, 0, 0)),
                  pl.BlockSpec((D, D), lambda i: (0, 0)),
                  pl.BlockSpec((1, D), lambda i: (0, 0))],
        out_specs=pl.BlockSpec((tm, D), lambda i: (i, 0)),
        out_shape=jax.ShapeDtypeStruct((N, D), F32),
        scratch_shapes=[pltpu.VMEM((tm, D), BF16)],
        compiler_params=_cparams(1),
        name="merge",
    )(*ys, pg2, x2, wb, wo, final_g.reshape(1, D))


def _rope_tables(pos):
    half = HEAD_DIM // 2
    inv = ROPE_BASE ** (-jnp.arange(half, dtype=F32) / half)
    ang = pos.astype(F32)[:, None] * inv[None, :]
    cos, sin = jnp.cos(ang), jnp.sin(ang)
    return jnp.concatenate([cos, cos], axis=1), jnp.concatenate([-sin, sin], axis=1)


def _wprep_kernel(a_ref, b_ref, g_ref, o_ref, og_ref, *, n_gate, plain_blocks, plain_rows, rows):
    j = pl.program_id(0)
    tn, D = a_ref.shape

    def copy(first_shifted):
        for r in range(0, tn, rows):
            if r < first_shifted:
                v = a_ref[r:r + rows, :]
            elif r + rows + n_gate <= tn:
                v = a_ref[r + n_gate:r + rows + n_gate, :]
            else:
                v = jnp.concatenate([a_ref[r + n_gate:tn, :], b_ref[...]], axis=0)
            o_ref[r:r + rows, :] = v.astype(BF16)

    @pl.when(j < plain_blocks)
    def _():
        copy(tn)

    @pl.when(j == plain_blocks)
    def _():
        copy(plain_rows)

    @pl.when(j > plain_blocks)
    def _():
        copy(0)

    @pl.when(j == 0)
    def _():
        og_ref[...] = jnp.concatenate([g_ref[...], jnp.zeros((LANES - n_gate, D), F32)], axis=0).astype(BF16)


def _wprep(wt_in, layer, H):
    _, n_in, D = wt_in.shape
    W = D // N_BRANCH
    n_gate = 2 * H
    assert n_gate == SUBLANES
    s = (ZB + 1) * W
    NC = n_in - n_gate
    tn = 2 * W
    rows = 256
    assert (s % tn) % rows == 0
    kern = functools.partial(_wprep_kernel, n_gate=n_gate, plain_blocks=s // tn, plain_rows=s % tn, rows=rows)
    return pl.pallas_call(
        kern,
        grid=(NC // tn,),
        in_specs=[pl.BlockSpec((None, tn, D), lambda j: (layer, j, 0)),
                  pl.BlockSpec((None, n_gate, D), lambda j: (layer, (j + 1) * (tn // n_gate), 0)),
                  pl.BlockSpec((None, n_gate, D), lambda j: (layer, s // n_gate, 0))],
        out_specs=[pl.BlockSpec((tn, D), lambda j: (j, 0)),
                   pl.BlockSpec((LANES, D), lambda j: (0, 0))],
        out_shape=[jax.ShapeDtypeStruct((NC, D), BF16), jax.ShapeDtypeStruct((LANES, D), BF16)],
        compiler_params=_cparams(1),
        name="wprep",
    )(wt_in, wt_in, wt_in)


def _prep_layer(wt_in, layer, b_i_l, b_f_l, w_branch_l, w_out_l, H):
    w_main, w_if = _wprep(wt_in, layer, H)
    bias_if = jnp.pad(jnp.concatenate([b_i_l, b_f_l]).astype(F32), (0, LANES - 2 * H)).reshape(1, LANES)
    return w_main, w_if, bias_if, w_branch_l.astype(BF16), w_out_l.astype(BF16)


def _trunk(x, past_len, cache_k, cache_v, C0, n0, m0, S0, conv0, layers, conv_w, conv_b, ln_g, ln_b,
           norm_g, final_g, cfg):
    B, T, D = x.shape
    N = B * T
    W = D // N_BRANCH
    H = W // HEAD_DIM
    depth = len(layers)
    cos2, sin2 = _rope_tables(past_len + jnp.arange(T))
    x2 = x.reshape(N, D)
    if cache_k is None:
        ck4 = cv4 = None
    else:
        ck4 = cache_k.reshape(depth, B, -1, HEAD_DIM)
        cv4 = cache_v.reshape(depth, B, -1, HEAD_DIM)
    states = []
    kv = None
    for l in range(depth):
        w_main, w_if, bias_if, wb, wo = layers[l]
        pg, k_all, v_all, ifp = _inproj(x2, norm_g[l], w_main, w_if, kv, l, depth, H, cfg["tm_in"])
        kv = (k_all, v_all)
        pg3 = pg.reshape(B, T, -1)
        ya = _stick_breaking(pg3, ck4, cv4, l, H, cfg["tq"], cfg["tk"])
        m0b = jnp.broadcast_to(m0[l].astype(F32)[..., None], (B, H, LANES))
        L = cfg["L"]
        (yb, C1, n1, m1), (yc, S1), (yd, conv1) = _mixers(
            [_mlstm_branch(pg3, ifp.reshape(B, T, LANES), bias_if, C0[l].astype(F32), n0[l].astype(F32), m0b, L),
             _retention_branch(pg3, cos2, sin2, S0[l].astype(F32), L),
             _conv_branch(pg3, conv0[l], conv_w[l], conv_b[l], ln_g[l], ln_b[l], L)],
            grid=(B, T // L))
        ys = tuple(y.reshape(N, W) for y in (ya, yb, yc, yd))
        x2 = _merge(ys, pg, x2, wb, wo, final_g, cfg["tm_mg"], final=(l == depth - 1))
        states.append((C1, n1, m1[..., 0], S1, conv1))
    stacked = tuple(jnp.stack(s, axis=0) for s in zip(*states))
    kv5 = tuple(a.reshape(depth, B, T, H, HEAD_DIM) for a in kv)
    return x2.reshape(B, T, D), kv5 + stacked


def _config(T):
    if T >= 256:
        return dict(tm_in=1024, tq=256, tk=256, L=256, tt=256, tm_mg=256)
    return dict(tm_in=512, tq=T, tk=128, L=T, tt=T, tm_mg=256)


def kernel(x_prompt, x_sample, cache_sb_k, cache_sb_v, state_mlstm_C, state_mlstm_n, state_mlstm_m,
           state_ret_S, state_conv, norm_g, w_in, mlstm_b_i, mlstm_b_f, conv_w, conv_b, conv_ln_g,
           conv_ln_b, w_branch, w_out, final_g):
    depth = w_in.shape[0]
    Bp, Tp, D = x_prompt.shape
    W = D // N_BRANCH
    H = W // HEAD_DIM
    wt_in = jnp.swapaxes(w_in, 1, 2)
    layers = [_prep_layer(wt_in, l, mlstm_b_i[l], mlstm_b_f[l], w_branch[l], w_out[l], H) for l in range(depth)]
    zC = jnp.zeros((depth, Bp, H, HEAD_DIM, HEAD_DIM), F32)
    zn = jnp.zeros((depth, Bp, H, HEAD_DIM), F32)
    zm = jnp.zeros((depth, Bp, H), F32)
    zconv = jnp.zeros((depth, Bp, CONV_WIDTH - 1, W), x_prompt.dtype)
    shared = (layers, conv_w, conv_b, conv_ln_g, conv_ln_b, norm_g, final_g)
    y_p, (pk, pv, pC, pn, pm, pS, pconv) = _trunk(
        x_prompt, 0, None, None, zC, zn, zm, zC, zconv, *shared, _config(Tp))
    y_s, (sk, sv, sC, sn, sm, sS, sconv) = _trunk(
        x_sample, cache_sb_k.shape[2], cache_sb_k, cache_sb_v, state_mlstm_C, state_mlstm_n,
        state_mlstm_m, state_ret_S, state_conv, *shared, _config(x_sample.shape[1]))
    return (y_p, y_s, pk, pv, pC, pn, pm, pS, pconv, sk, sv, sC, sn, sm, sS, sconv)
```

```python
import functools

import numpy as np
import jax
import jax.numpy as jnp
from jax import lax
from jax.experimental import pallas as pl
from jax.experimental.pallas import tpu as pltpu

F32 = jnp.float32
BF16 = jnp.bfloat16

HEAD_DIM = 128
N_BRANCH = 4
CONV_WIDTH = 31
ROPE_BASE = 10000.0
NORM_EPS = 1e-6
LN_EPS = 1e-5
LANES = 128
SUBLANES = 8
HALO = 32
NEG = -1e30
SB_CUT = 104.0
SB_CACHE_GROUP = 4
VMEM_LIMIT = 56 * 1024 * 1024

QA, KA, VA, ZA, QB, KB, VB, OB, ZB, QC, KC, VC, ZC, UD, GD, ZD = range(16)


def _cparams(n_axes):
    return pltpu.CompilerParams(dimension_semantics=("arbitrary",) * n_axes,
                                vmem_limit_bytes=VMEM_LIMIT)


def _dot(a, b):
    return jnp.dot(a, b, preferred_element_type=F32)


def _dot_nt(a, b):
    return lax.dot_general(a, b, (((1,), (1,)), ((), ())), preferred_element_type=F32)


def _dot_tn(a, b):
    return lax.dot_general(a, b, (((0,), (0,)), ((), ())), preferred_element_type=F32)


def _softplus(x):
    return jnp.maximum(x, 0.0) + jnp.log1p(jnp.exp(-jnp.abs(x)))


def _softplus_plain_log(x):
    return jnp.maximum(x, 0.0) + jnp.log(1.0 + jnp.exp(-jnp.abs(x)))


def _log_sigmoid(x):
    return -_softplus(-x)


def _sigmoid(x):
    return 1.0 / (1.0 + jnp.exp(-x))


def _silu(x):
    return x * _sigmoid(x)


def _head_norm(x):
    mu = jnp.mean(x, axis=-1, keepdims=True)
    xc = x - mu
    return xc * lax.rsqrt(jnp.mean(xc * xc, axis=-1, keepdims=True) + LN_EPS)


def _head_norm_mxu(x):
    j = jnp.full((HEAD_DIM, HEAD_DIM), 1.0 / HEAD_DIM, BF16)
    xc = x - _split_dot(x, j, 2)
    return xc * lax.rsqrt(_split_dot(xc * xc, j, 2) + LN_EPS)


def _run_staged(stage_iters, lead):
    live = list(stage_iters)
    for it, n in zip(live, lead):
        for _ in range(n):
            next(it, None)
    while live:
        for it in list(live):
            if next(it, StopIteration) is StopIteration:
                live.remove(it)


def _split_dot(x, m, terms):
    out = None
    r = x
    for t in range(terms):
        p = r.astype(BF16)
        d = _dot(p, m)
        out = d if out is None else out + d
        if t + 1 < terms:
            r = r - p.astype(F32)
    return out


def _pad_rows(x, rows):
    if x.shape[0] == rows:
        return x
    return jnp.concatenate([x, jnp.zeros((rows - x.shape[0],) + x.shape[1:], x.dtype)], axis=0)


def _inproj_kernel(x_ref, g_ref, w_ref, wif_ref, kprev_ref, vprev_ref, pg_ref, k_ref, v_ref, if_ref, hin_ref,
                   *, W, H, rows):
    del kprev_ref, vprev_ref
    j = pl.program_id(1)
    tm = x_ref.shape[0]

    @pl.when(j == 0)
    def _():
        for r in range(0, tm, rows):
            x = x_ref[r:r + rows, :]
            ms = jnp.mean(x * x, axis=-1, keepdims=True)
            hb = ((x * lax.rsqrt(ms + NORM_EPS)) * g_ref[...]).astype(BF16)
            hin_ref[r:r + rows, :] = hb
            if_ref[r:r + rows, :] = _dot_nt(hb, wif_ref[...])

    acc = _dot_nt(hin_ref[...], w_ref[...])
    pg_ref[...] = acc.astype(BF16)

    def put_state(ref, col0):
        for h in range(H):
            ref[pl.ds(h, tm, stride=H), :] = acc[:, col0 + h * HEAD_DIM:col0 + (h + 1) * HEAD_DIM]

    @pl.when(j == 0)
    def _():
        put_state(k_ref, KA * W)

    @pl.when(j == 1)
    def _():
        put_state(v_ref, (VA - 2) * W)


def _inproj(x2, norm_g, wt_main, wt_if, kv_prev, layer, depth, H, tm):
    N, D = x2.shape
    NC = wt_main.shape[0]
    W = D // N_BRANCH
    tn = 2 * W
    rows = min(tm, 256)
    kern = functools.partial(_inproj_kernel, W=W, H=H, rows=rows)
    kv_shape = jax.ShapeDtypeStruct((depth, N * H, HEAD_DIM), F32)
    in_specs = [
        pl.BlockSpec((tm, D), lambda i, j: (i, 0)),
        pl.BlockSpec((1, D), lambda i, j: (0, 0)),
        pl.BlockSpec((tn, D), lambda i, j: (j, 0)),
        pl.BlockSpec((LANES, D), lambda i, j: (0, 0)),
    ]
    if kv_prev is None:
        kv_prev = (jnp.zeros(kv_shape.shape, F32), jnp.zeros(kv_shape.shape, F32))
    in_specs += [pl.BlockSpec(memory_space=pl.ANY)] * 2
    aliases = {4: 1, 5: 2}
    kv_spec = pl.BlockSpec((None, tm * H, HEAD_DIM), lambda i, j: (layer, i, 0))
    return pl.pallas_call(
        kern,
        grid=(N // tm, NC // tn),
        in_specs=in_specs,
        out_specs=[
            pl.BlockSpec((tm, tn), lambda i, j: (i, j)),
            kv_spec,
            kv_spec,
            pl.BlockSpec((tm, LANES), lambda i, j: (i, 0)),
        ],
        out_shape=[jax.ShapeDtypeStruct((N, NC), BF16), kv_shape, kv_shape,
                   jax.ShapeDtypeStruct((N, LANES), F32)],
        scratch_shapes=[pltpu.VMEM((tm, D), BF16)],
        input_output_aliases=aliases,
        compiler_params=_cparams(2),
        name="inproj",
    )(x2, norm_g.reshape(1, D), wt_main, wt_if, *kv_prev)


SB_SCALE = HEAD_DIM ** -0.5


def _strict_upper(n):
    return (lax.broadcasted_iota(jnp.int32, (n, n), 0)
            > lax.broadcasted_iota(jnp.int32, (n, n), 1)).astype(BF16)


def _sb_new_rows(q_ref, kn_ref, vn_ref, acc_ref, car_ref, units, tq):
    tkd = max(tq, LANES)
    u_new = _strict_upper(tkd)
    causal = (lax.broadcasted_iota(jnp.int32, (tq, tkd), 1)
              < lax.broadcasted_iota(jnp.int32, (tq, tkd), 0))
    zs = [_dot_nt(q_ref[g, :, sl], _pad_rows(kn_ref[g, :, sl], tkd)) * SB_SCALE for g, sl in units]
    sps = [jnp.where(causal, _softplus_plain_log(z), 0.0) for z in zs]
    cums = [_split_dot(sp, u_new, 2) for sp in sps]
    ws = [jnp.where(causal, jnp.exp(z - sp - cum), 0.0).astype(BF16) for z, sp, cum in zip(zs, sps, cums)]
    for u, (g, sl) in enumerate(units):
        acc_ref[g, :, sl] = _dot(ws[u], _pad_rows(vn_ref[g, :, sl], tkd))
        car_ref[u] = jnp.sum(sps[u], axis=1, keepdims=True)


def _sb_past_block(q_ref, k_of, v_of, acc_ref, car_ref, units, u_past):
    zb = [_dot_nt(q_ref[g, :, sl], k_of(u)) * SB_SCALE for u, (g, sl) in enumerate(units)]
    spb = [_softplus_plain_log(z) for z in zb]
    cumb = [_split_dot(sp, u_past, 2) for sp in spb]
    cars = [car_ref[u] for u in range(len(units))]
    wb = [jnp.exp(z - sp - cum - car).astype(BF16) for z, sp, cum, car in zip(zb, spb, cumb, cars)]
    lowest = None
    for u, (g, sl) in enumerate(units):
        acc_ref[g, :, sl] += _dot(wb[u], v_of(u))
        car_new = cars[u] + jnp.sum(spb[u], axis=1, keepdims=True)
        car_ref[u] = car_new
        lowest = car_new if lowest is None else jnp.minimum(lowest, car_new)
    return jnp.min(lowest)


def _sb_units(G, H):
    return [(g, slice(h * HEAD_DIM, (h + 1) * HEAD_DIM)) for g in range(G) for h in range(H)]


def _sb_kernel(q_ref, kn_ref, vn_ref, z_ref, kp_ref, vp_ref, o_ref, acc_ref, car_ref, *, tq, tk, H):
    qi = pl.program_id(1)
    units = _sb_units(1, H)
    _sb_new_rows(q_ref, kn_ref, vn_ref, acc_ref, car_ref, units, tq)
    u_past = _strict_upper(tk)

    def cond(c):
        kb, go = c
        return jnp.logical_and(kb >= 0, go)

    def body(c):
        kb, _ = c
        start = pl.multiple_of(kb * tk, tk)
        lowest = _sb_past_block(q_ref, lambda u: kp_ref[pl.ds(start, tk), units[u][1]],
                                lambda u: vp_ref[pl.ds(start, tk), units[u][1]], acc_ref, car_ref, units, u_past)
        return kb - 1, lowest < SB_CUT

    lax.while_loop(cond, body, ((qi * tq) // tk - 1, True))
    o_ref[...] = (acc_ref[...] * _silu(z_ref[...].astype(F32))).astype(BF16)


def _sb_cache_kernel(q_ref, kn_ref, vn_ref, z_ref, kc_ref, vc_ref, o_ref, acc_ref, car_ref, kbuf, vbuf, sem,
                     *, tq, tk, G, H, layer, n_blocks):
    b0 = pl.program_id(0) * G
    units = _sb_units(G, H)
    rows = tk * H

    def copies(i):
        src = pl.ds((n_blocks - 1 - i) * rows, rows)
        slot = i % 2
        out = []
        for g in range(G):
            out.append(pltpu.make_async_copy(kc_ref.at[layer, b0 + g, src, :], kbuf.at[slot, g], sem.at[0, slot, g]))
            out.append(pltpu.make_async_copy(vc_ref.at[layer, b0 + g, src, :], vbuf.at[slot, g], sem.at[1, slot, g]))
        return out

    for c in copies(0):
        c.start()
    _sb_new_rows(q_ref, kn_ref, vn_ref, acc_ref, car_ref, units, tq)
    u_past = _strict_upper(tk)

    def cond(c):
        i, go = c
        return jnp.logical_and(i < n_blocks, go)

    def body(c):
        i, _ = c
        slot = i % 2
        for cp in copies(i):
            cp.wait()

        @pl.when(i + 1 < n_blocks)
        def _():
            for cp in copies(i + 1):
                cp.start()

        head_rows = lambda buf, u: buf[slot, u // H, pl.ds(u % H, tk, stride=H), :].astype(BF16)
        lowest = _sb_past_block(q_ref, lambda u: head_rows(kbuf, u), lambda u: head_rows(vbuf, u),
                                acc_ref, car_ref, units, u_past)
        return i + 1, lowest < SB_CUT

    done, _ = lax.while_loop(cond, body, (0, True))

    @pl.when(done < n_blocks)
    def _():
        for cp in copies(done):
            cp.wait()

    o_ref[...] = (acc_ref[...] * _silu(z_ref[...].astype(F32))).astype(BF16)


def _stick_breaking(pg3, cache_k, cache_v, layer, H, tq, tk):
    B, T, NC = pg3.shape
    W = H * HEAD_DIM
    if cache_k is None:
        G = 1
        kp, vp = pg3, pg3
        kp_spec = pl.BlockSpec((None, T, W), lambda b, i: (b, 0, KA), pipeline_mode=pl.Buffered(1))
        vp_spec = pl.BlockSpec((None, T, W), lambda b, i: (b, 0, VA), pipeline_mode=pl.Buffered(1))
        kern = functools.partial(_sb_kernel, tq=tq, tk=tk, H=H)
        extra = []
    else:
        assert T == tq and cache_k.shape[2] % (tk * H) == 0 and cache_k.shape[2] > 0
        G = SB_CACHE_GROUP if B % SB_CACHE_GROUP == 0 else 1
        kp, vp = cache_k, cache_v
        kp_spec = vp_spec = pl.BlockSpec(memory_space=pl.ANY)
        kern = functools.partial(_sb_cache_kernel, tq=tq, tk=tk, G=G, H=H, layer=layer,
                                 n_blocks=cache_k.shape[2] // (tk * H))
        extra = [pltpu.VMEM((2, G, tk * H, HEAD_DIM), F32), pltpu.VMEM((2, G, tk * H, HEAD_DIM), F32),
                 pltpu.SemaphoreType.DMA((2, 2, G))]
    blk = lambda g: pl.BlockSpec((G, tq, W), lambda b, i, g=g: (b, i, g))
    return pl.pallas_call(
        kern,
        grid=(B // G, T // tq),
        in_specs=[blk(QA), blk(KA), blk(VA), blk(ZA), kp_spec, vp_spec],
        out_specs=pl.BlockSpec((G, tq, W), lambda b, i: (b, i, 0)),
        out_shape=jax.ShapeDtypeStruct((B, T, W), BF16),
        scratch_shapes=[pltpu.VMEM((G, tq, W), F32), pltpu.VMEM((G * H, tq, 1), F32)] + extra,
        compiler_params=_cparams(2),
        name="stick_breaking",
    )(pg3, pg3, pg3, pg3, kp, vp)


def _mlstm_stages(q_ref, k_ref, v_ref, o_ref, z_ref, if_ref, bias_ref, c0_ref, n0_ref, m0_ref,
                  y_ref, c_ref, n_ref, m_ref, *, L, Lc, H):
    ci = pl.program_id(1)
    scale = HEAD_DIM ** -0.5

    @pl.when(ci == 0)
    def _():
        c_ref[...] = c0_ref[...]
        n_ref[...] = n0_ref[...]
        m_ref[...] = m0_ref[...]

    yield
    ifv = _pad_rows(if_ref[...] + bias_ref[...], Lc)
    valid = lax.broadcasted_iota(jnp.int32, (Lc, LANES), 0) < L
    ipre = jnp.where(valid, ifv, NEG)
    logf = jnp.where(valid, _log_sigmoid(ifv), 0.0)
    tri = (lax.broadcasted_iota(jnp.int32, (Lc, Lc), 1)
           <= lax.broadcasted_iota(jnp.int32, (Lc, Lc), 0))
    b_all = _split_dot_left(tri.astype(BF16), logf, 3)
    c_all = ipre - pltpu.roll(b_all, LANES - H, axis=1)
    c_all_t = c_all.T

    hs = range(H)
    heads = [slice(h * HEAD_DIM, (h + 1) * HEAD_DIM) for h in hs]
    qh = [_pad_rows(q_ref[:, sl], Lc) for sl in heads]
    kh = [_pad_rows(k_ref[:, sl], Lc) for sl in heads]
    vh = [_pad_rows(v_ref[:, sl], Lc) for sl in heads]
    C = [c_ref[h] for h in hs]
    n_row = [n_ref[h:h + 1, :] for h in hs]
    m_prev = [m_ref[h:h + 1, 0:1] for h in hs]
    s_qk = [_dot_nt(qh[h], kh[h]) * scale for h in hs]
    q_c = [_dot(qh[h], C[h].astype(BF16)) for h in hs]

    yield
    bcol = [b_all[:, H + h:H + h + 1] for h in hs]
    b_last = [bcol[h][Lc - 1:Lc, :] for h in hs]
    gcol = [b_last[h] + c_all[:, h:h + 1] for h in hs]
    m_new = [jnp.maximum(m_prev[h] + b_last[h], jnp.max(gcol[h], axis=0, keepdims=True)) for h in hs]
    decay = [jnp.exp(m_prev[h] + b_last[h] - m_new[h]) for h in hs]
    kw = [kh[h].astype(F32) * (scale * jnp.exp(gcol[h] - m_new[h])) for h in hs]
    kw_v = [_dot_tn(kw[h].astype(BF16), vh[h]) for h in hs]
    for h in hs:
        c_ref[h] = decay[h] * C[h] + kw_v[h]
        n_ref[h:h + 1, :] = decay[h] * n_row[h] + jnp.sum(kw[h], axis=0, keepdims=True)
        m_ref[h:h + 1, :] = jnp.broadcast_to(m_new[h], (1, LANES))

    yield
    inter = [m_prev[h] + bcol[h] for h in hs]
    m_row, w, w_v = [], [], []
    for h in hs:
        logD = jnp.where(tri, bcol[h] + c_all_t[h:h + 1, :], NEG)
        m_row.append(jnp.maximum(inter[h], jnp.max(logD, axis=1, keepdims=True)))
        w.append(jnp.exp(logD - m_row[h]) * s_qk[h])
        w_v.append(_dot(w[h].astype(BF16), vh[h]))
        yield
    hh = []
    for h in hs:
        a_inter = jnp.exp(inter[h] - m_row[h])
        num = w_v[h] + a_inter * q_c[h]
        qn = jnp.sum(qh[h].astype(F32) * n_row[h], axis=1, keepdims=True)
        den = jnp.sum(w[h], axis=1, keepdims=True) + a_inter * qn
        hh.append(num / jnp.maximum(jnp.abs(den), jnp.exp(-m_row[h])))
        yield
    hn = [_head_norm_mxu(x) for x in hh]
    for h, sl in enumerate(heads):
        og = _sigmoid(o_ref[:, sl].astype(F32))
        zg = _silu(z_ref[:, sl].astype(F32))
        y_ref[:, sl] = ((hn[h][:L] * og) * zg).astype(BF16)
        yield


def _split_dot_left(m, x, terms):
    out = None
    r = x
    for t in range(terms):
        p = r.astype(BF16)
        d = _dot(m, p)
        out = d if out is None else out + d
        if t + 1 < terms:
            r = r - p.astype(F32)
    return out


class _Branch:
    def __init__(self, args, in_specs, out_specs, out_shapes, scratch, stages, lead=0):
        self.args, self.in_specs, self.out_specs = args, in_specs, out_specs
        self.out_shapes, self.scratch, self.stages = out_shapes, scratch, stages
        self.lead = lead


def _mlstm_branch(pg3, if3, bias_if, C0, n0, m0, L):
    B, T, NC = pg3.shape
    H = C0.shape[1]
    W = H * HEAD_DIM
    Lc = max(L, LANES)
    blk = lambda g: pl.BlockSpec((None, L, W), lambda b, c, g=g: (b, c, g))
    st4 = pl.BlockSpec((None, H, HEAD_DIM, HEAD_DIM), lambda b, c: (b, 0, 0, 0))
    st3 = pl.BlockSpec((None, H, LANES), lambda b, c: (b, 0, 0))
    return _Branch(
        args=[pg3, pg3, pg3, pg3, pg3, if3, bias_if, C0, n0, m0],
        in_specs=[blk(QB), blk(KB), blk(VB), blk(OB), blk(ZB),
                  pl.BlockSpec((None, L, LANES), lambda b, c: (b, c, 0)),
                  pl.BlockSpec((1, LANES), lambda b, c: (0, 0)),
                  st4, st3, st3],
        out_specs=[pl.BlockSpec((None, L, W), lambda b, c: (b, c, 0)), st4, st3, st3],
        out_shapes=[jax.ShapeDtypeStruct((B, T, W), BF16),
                    jax.ShapeDtypeStruct((B, H, HEAD_DIM, HEAD_DIM), F32),
                    jax.ShapeDtypeStruct((B, H, LANES), F32),
                    jax.ShapeDtypeStruct((B, H, LANES), F32)],
        scratch=[],
        stages=functools.partial(_mlstm_stages, L=L, Lc=Lc, H=H))


def _ret_stages(q_ref, k_ref, v_ref, z_ref, cos_ref, sin_ref, dm_ref, qd_ref, kd_ref, s0_ref,
                y_ref, s_ref, *, L, Lc, H, chunk_decay):
    ci = pl.program_id(1)
    scale = HEAD_DIM ** -0.5

    @pl.when(ci == 0)
    def _():
        s_ref[...] = s0_ref[...]

    yield
    cos2 = _pad_rows(cos_ref[...], Lc)
    sin2 = _pad_rows(sin_ref[...], Lc)

    def rot(x):
        return x * cos2 + pltpu.roll(x, HEAD_DIM // 2, axis=1) * sin2

    hs = range(H)
    heads = [slice(h * HEAD_DIM, (h + 1) * HEAD_DIM) for h in hs]
    qb = [rot(_pad_rows(q_ref[:, sl], Lc).astype(F32)).astype(BF16) for sl in heads]
    kr = [rot(_pad_rows(k_ref[:, sl], Lc).astype(F32)) * scale for sl in heads]
    vh = [_pad_rows(v_ref[:, sl], Lc) for sl in heads]
    S = [s_ref[h] for h in hs]
    qk = [_dot_nt(qb[h], kr[h].astype(BF16)) for h in hs]
    q_s = [_dot(qb[h], S[h].astype(BF16)) for h in hs]
    yield
    kd_v = [_dot_tn((kr[h] * kd_ref[:, h:h + 1]).astype(BF16), vh[h]) for h in hs]
    for h in hs:
        s_ref[h] = chunk_decay[h] * S[h] + kd_v[h]
    yield
    att_v = [_dot((qk[h] * dm_ref[h]).astype(BF16), vh[h]) for h in hs]
    yield
    hn = [_head_norm_mxu(att_v[h] + q_s[h] * qd_ref[:, h:h + 1]) for h in hs]
    for h, sl in enumerate(heads):
        zg = _silu(z_ref[:, sl].astype(F32))
        y_ref[:, sl] = (hn[h][:L] * zg).astype(BF16)


def _retention_tables(H, L, Lc):
    lg = np.log1p(-np.exp2(-5.0 - np.arange(H, dtype=np.float64)))
    t = np.arange(Lc, dtype=np.float64)
    rel = t[:, None] - t[None, :]
    dmask = np.where(rel >= 0, np.exp(lg[:, None, None] * np.maximum(rel, 0.0)), 0.0)
    qd = np.zeros((Lc, LANES))
    kd = np.zeros((Lc, LANES))
    qd[:, :H] = np.exp(lg[None, :] * (t[:, None] + 1.0))
    kd[:, :H] = np.where(t[:, None] < L, np.exp(lg[None, :] * (L - 1.0 - t[:, None])), 0.0)
    chunk_decay = tuple(float(v) for v in np.exp(lg * L))
    return (jnp.asarray(dmask, F32), jnp.asarray(qd, F32), jnp.asarray(kd, F32), chunk_decay)


def _retention_branch(pg3, cos2, sin2, S0, L):
    B, T, NC = pg3.shape
    H = S0.shape[1]
    W = H * HEAD_DIM
    Lc = max(L, LANES)
    dmask, qd, kd, chunk_decay = _retention_tables(H, L, Lc)
    blk = lambda g: pl.BlockSpec((None, L, W), lambda b, c, g=g: (b, c, g))
    st4 = pl.BlockSpec((None, H, HEAD_DIM, HEAD_DIM), lambda b, c: (b, 0, 0, 0))
    tab = pl.BlockSpec((L, HEAD_DIM), lambda b, c: (c, 0))
    return _Branch(
        args=[pg3, pg3, pg3, pg3, cos2, sin2, dmask, qd, kd, S0],
        in_specs=[blk(QC), blk(KC), blk(VC), blk(ZC), tab, tab,
                  pl.BlockSpec((H, Lc, Lc), lambda b, c: (0, 0, 0)),
                  pl.BlockSpec((Lc, LANES), lambda b, c: (0, 0)),
                  pl.BlockSpec((Lc, LANES), lambda b, c: (0, 0)),
                  st4],
        out_specs=[pl.BlockSpec((None, L, W), lambda b, c: (b, c, 0)), st4],
        out_shapes=[jax.ShapeDtypeStruct((B, T, W), BF16),
                    jax.ShapeDtypeStruct((B, H, HEAD_DIM, HEAD_DIM), F32)],
        scratch=[],
        stages=functools.partial(_ret_stages, L=L, Lc=Lc, H=H, chunk_decay=chunk_decay))


def _conv_stages(u_ref, g_ref, z_ref, c0_ref, w_ref, b_ref, lg_ref, lb_ref, y_ref, c1_ref, xs_ref, cv_ref,
                 sh_ref, *, tt, rows):
    ti = pl.program_id(1)
    hist = CONV_WIDTH - 1
    off = HALO - hist
    W = xs_ref.shape[1]

    @pl.when(ti == 0)
    def _():
        xs_ref[0:HALO, :] = jnp.zeros((HALO, W), F32)
        xs_ref[off:HALO, :] = c0_ref[...]

    @pl.when(ti > 0)
    def _():
        xs_ref[0:HALO, :] = xs_ref[tt:tt + HALO, :]

    yield
    for r in range(0, tt, rows):
        xs_ref[HALO + r:HALO + r + rows, :] = (u_ref[r:r + rows, :].astype(F32)
                                               * _sigmoid(g_ref[r:r + rows, :].astype(F32)))
    c1_ref[...] = xs_ref[tt + off:tt + HALO, :]

    for s in range(SUBLANES):
        span = tt + (CONV_WIDTH - 1 - s) // SUBLANES * SUBLANES
        sh_ref[s, 0:span, :] = xs_ref[off + s:off + s + span, :]

    yield
    for r in range(0, tt, rows):
        for c in range(0, W, LANES):
            acc = jnp.broadcast_to(b_ref[:, c:c + LANES], (rows, LANES))
            for j in range(CONV_WIDTH):
                a, s = j // SUBLANES * SUBLANES, j % SUBLANES
                acc = acc + w_ref[j:j + 1, c:c + LANES] * sh_ref[s, r + a:r + a + rows, c:c + LANES]
            cv_ref[r:r + rows, c:c + LANES] = acc
            yield
        ln = _head_norm(cv_ref[r:r + rows, :]) * lg_ref[...] + lb_ref[...]
        y_ref[r:r + rows, :] = (_silu(ln) * _silu(z_ref[r:r + rows, :].astype(F32))).astype(BF16)
        yield


def _conv_branch(pg3, conv0, conv_w, conv_b, ln_g, ln_b, tt):
    B, T, NC = pg3.shape
    W = conv0.shape[-1]
    hist = CONV_WIDTH - 1
    rows = min(tt, 64)
    blk = lambda g: pl.BlockSpec((None, tt, W), lambda b, t, g=g: (b, t, g))
    vec = pl.BlockSpec((1, W), lambda b, t: (0, 0))
    return _Branch(
        args=[pg3, pg3, pg3, conv0, conv_w, conv_b.reshape(1, W), ln_g.reshape(1, W), ln_b.reshape(1, W)],
        in_specs=[blk(UD), blk(GD), blk(ZD),
                  pl.BlockSpec((None, hist, W), lambda b, t: (b, 0, 0)),
                  pl.BlockSpec((CONV_WIDTH, W), lambda b, t: (0, 0)),
                  vec, vec, vec],
        out_specs=[pl.BlockSpec((None, tt, W), lambda b, t: (b, t, 0)),
                   pl.BlockSpec((None, hist, W), lambda b, t: (b, 0, 0))],
        out_shapes=[jax.ShapeDtypeStruct((B, T, W), BF16),
                    jax.ShapeDtypeStruct((B, hist, W), F32)],
        scratch=[pltpu.VMEM((tt + HALO, W), F32), pltpu.VMEM((tt, W), F32),
                 pltpu.VMEM((SUBLANES, tt + HALO - SUBLANES, W), F32)],
        stages=functools.partial(_conv_stages, tt=tt, rows=rows),
        lead=2)


def _mixers_kernel(*refs, branches):
    n_in = sum(len(b.in_specs) for b in branches)
    n_out = sum(len(b.out_specs) for b in branches)
    ins, outs, scr = list(refs[:n_in]), list(refs[n_in:n_in + n_out]), list(refs[n_in + n_out:])
    iters = []
    for b in branches:
        take = lambda pool, n: [pool.pop(0) for _ in range(n)]
        iters.append(b.stages(*take(ins, len(b.in_specs)), *take(outs, len(b.out_specs)),
                              *take(scr, len(b.scratch))))
    _run_staged(iters, [b.lead for b in branches])


def _mixers(branches, grid):
    outs = pl.pallas_call(
        functools.partial(_mixers_kernel, branches=branches),
        grid=grid,
        in_specs=[s for b in branches for s in b.in_specs],
        out_specs=[s for b in branches for s in b.out_specs],
        out_shape=[s for b in branches for s in b.out_shapes],
        scratch_shapes=[s for b in branches for s in b.scratch],
        compiler_params=_cparams(len(grid)),
        name="mixers",
    )(*[a for b in branches for a in b.args])
    split, k = [], 0
    for b in branches:
        split.append(outs[k:k + len(b.out_specs)])
        k += len(b.out_specs)
    return split


def _merge_kernel(ya_ref, yb_ref, yc_ref, yd_ref, g_ref, x_ref, wb_ref, wo_ref, fg_ref, o_ref, mg_ref,
                  *, D, W, final):
    ys = (ya_ref, yb_ref, yc_ref, yd_ref)
    for c in range(0, D, W):
        acc = None
        for i in range(N_BRANCH):
            p = _dot(ys[i][...], wb_ref[i, :, c:c + W])
            t = _sigmoid(g_ref[:, i * D + c:i * D + c + W].astype(F32)) * p
            acc = t if acc is None else acc + t
        mg_ref[:, c:c + W] = acc.astype(BF16)
    o = x_ref[...] + _dot(mg_ref[...], wo_ref[...])
    if final:
        ms = jnp.mean(o * o, axis=-1, keepdims=True)
        o = (o * lax.rsqrt(ms + NORM_EPS)) * fg_ref[...]
    o_ref[...] = o


def _merge(ys, pg2, x2, wb, wo, final_g, tm, final):
    N, D = x2.shape
    W = D // N_BRANCH
    yblk = pl.BlockSpec((tm, W), lambda i: (i, 0))
    kern = functools.partial(_merge_kernel, D=D, W=W, final=final)
    return pl.pallas_call(
        kern,
        grid=(N // tm,),
        in_specs=[yblk, yblk, yblk, yblk,
                  pl.BlockSpec((tm, N_BRANCH * D), lambda i: (i, 1)),
                  pl.BlockSpec((tm, D), lambda i: (i, 0)),
                  pl.BlockSpec((N_BRANCH, W, D), lambda i: (0, 0, 0)),
                  pl.BlockSpec((D, D), lambda i: (0, 0)),
                  pl.BlockSpec((1, D), lambda i: (0, 0))],
        out_specs=pl.BlockSpec((tm, D), lambda i: (i, 0)),
        out_shape=jax.ShapeDtypeStruct((N, D), F32),
        scratch_shapes=[pltpu.VMEM((tm, D), BF16)],
        compiler_params=_cparams(1),
        name="merge",
    )(*ys, pg2, x2, wb, wo, final_g.reshape(1, D))


def _rope_tables(pos):
    half = HEAD_DIM // 2
    inv = ROPE_BASE ** (-jnp.arange(half, dtype=F32) / half)
    ang = pos.astype(F32)[:, None] * inv[None, :]
    cos, sin = jnp.cos(ang), jnp.sin(ang)
    return jnp.concatenate([cos, cos], axis=1), jnp.concatenate([-sin, sin], axis=1)


def _wprep_kernel(a_ref, b_ref, g_ref, o_ref, og_ref, *, n_gate, plain_blocks, plain_rows, rows):
    j = pl.program_id(0)
    tn, D = a_ref.shape

    def copy(first_shifted):
        for r in range(0, tn, rows):
            if r < first_shifted:
                v = a_ref[r:r + rows, :]
            elif r + rows + n_gate <= tn:
                v = a_ref[r + n_gate:r + rows + n_gate, :]
            else:
                v = jnp.concatenate([a_ref[r + n_gate:tn, :], b_ref[...]], axis=0)
            o_ref[r:r + rows, :] = v.astype(BF16)

    @pl.when(j < plain_blocks)
    def _():
        copy(tn)

    @pl.when(j == plain_blocks)
    def _():
        copy(plain_rows)

    @pl.when(j > plain_blocks)
    def _():
        copy(0)

    @pl.when(j == 0)
    def _():
        og_ref[...] = jnp.concatenate([g_ref[...], jnp.zeros((LANES - n_gate, D), F32)], axis=0).astype(BF16)


def _wprep(wt_in, layer, H):
    _, n_in, D = wt_in.shape
    W = D // N_BRANCH
    n_gate = 2 * H
    assert n_gate == SUBLANES
    s = (ZB + 1) * W
    NC = n_in - n_gate
    tn = 2 * W
    rows = 256
    assert (s % tn) % rows == 0
    kern = functools.partial(_wprep_kernel, n_gate=n_gate, plain_blocks=s // tn, plain_rows=s % tn, rows=rows)
    return pl.pallas_call(
        kern,
        grid=(NC // tn,),
        in_specs=[pl.BlockSpec((None, tn, D), lambda j: (layer, j, 0)),
                  pl.BlockSpec((None, n_gate, D), lambda j: (layer, (j + 1) * (tn // n_gate), 0)),
                  pl.BlockSpec((None, n_gate, D), lambda j: (layer, s // n_gate, 0))],
        out_specs=[pl.BlockSpec((tn, D), lambda j: (j, 0)),
                   pl.BlockSpec((LANES, D), lambda j: (0, 0))],
        out_shape=[jax.ShapeDtypeStruct((NC, D), BF16), jax.ShapeDtypeStruct((LANES, D), BF16)],
        compiler_params=_cparams(1),
        name="wprep",
    )(wt_in, wt_in, wt_in)


def _prep_layer(wt_in, layer, b_i_l, b_f_l, w_branch_l, w_out_l, H):
    w_main, w_if = _wprep(wt_in, layer, H)
    bias_if = jnp.pad(jnp.concatenate([b_i_l, b_f_l]).astype(F32), (0, LANES - 2 * H)).reshape(1, LANES)
    return w_main, w_if, bias_if, w_branch_l.astype(BF16), w_out_l.astype(BF16)


def _trunk(x, past_len, cache_k, cache_v, C0, n0, m0, S0, conv0, layers, conv_w, conv_b, ln_g, ln_b,
           norm_g, final_g, cfg):
    B, T, D = x.shape
    N = B * T
    W = D // N_BRANCH
    H = W // HEAD_DIM
    depth = len(layers)
    cos2, sin2 = _rope_tables(past_len + jnp.arange(T))
    x2 = x.reshape(N, D)
    if cache_k is None:
        ck4 = cv4 = None
    else:
        ck4 = cache_k.reshape(depth, B, -1, HEAD_DIM)
        cv4 = cache_v.reshape(depth, B, -1, HEAD_DIM)
    states = []
    kv = None
    for l in range(depth):
        w_main, w_if, bias_if, wb, wo = layers[l]
        pg, k_all, v_all, ifp = _inproj(x2, norm_g[l], w_main, w_if, kv, l, depth, H, cfg["tm_in"])
        kv = (k_all, v_all)
        pg3 = pg.reshape(B, T, -1)
        ya = _stick_breaking(pg3, ck4, cv4, l, H, cfg["tq"], cfg["tk"])
        m0b = jnp.broadcast_to(m0[l].astype(F32)[..., None], (B, H, LANES))
        L = cfg["L"]
        (yb, C1, n1, m1), (yc, S1), (yd, conv1) = _mixers(
            [_mlstm_branch(pg3, ifp.reshape(B, T, LANES), bias_if, C0[l].astype(F32), n0[l].astype(F32), m0b, L),
             _retention_branch(pg3, cos2, sin2, S0[l].astype(F32), L),
             _conv_branch(pg3, conv0[l], conv_w[l], conv_b[l], ln_g[l], ln_b[l], L)],
            grid=(B, T // L))
        ys = tuple(y.reshape(N, W) for y in (ya, yb, yc, yd))
        x2 = _merge(ys, pg, x2, wb, wo, final_g, cfg["tm_mg"], final=(l == depth - 1))
        states.append((C1, n1, m1[..., 0], S1, conv1))
    stacked = tuple(jnp.stack(s, axis=0) for s in zip(*states))
    kv5 = tuple(a.reshape(depth, B, T, H, HEAD_DIM) for a in kv)
    return x2.reshape(B, T, D), kv5 + stacked


def _config(T):
    if T >= 256:
        return dict(tm_in=1024, tq=256, tk=256, L=256, tt=256, tm_mg=256)
    return dict(tm_in=512, tq=T, tk=128, L=T, tt=T, tm_mg=256)


def kernel(x_prompt, x_sample, cache_sb_k, cache_sb_v, state_mlstm_C, state_mlstm_n, state_mlstm_m,
           state_ret_S, state_conv, norm_g, w_in, mlstm_b_i, mlstm_b_f, conv_w, conv_b, conv_ln_g,
           conv_ln_b, w_branch, w_out, final_g):
    depth = w_in.shape[0]
    Bp, Tp, D = x_prompt.shape
    W = D // N_BRANCH
    H = W // HEAD_DIM
    wt_in = jnp.swapaxes(w_in, 1, 2)
    layers = [_prep_layer(wt_in, l, mlstm_b_i[l], mlstm_b_f[l], w_branch[l], w_out[l], H) for l in range(depth)]
    zC = jnp.zeros((depth, Bp, H, HEAD_DIM, HEAD_DIM), F32)
    zn = jnp.zeros((depth, Bp, H, HEAD_DIM), F32)
    zm = jnp.zeros((depth, Bp, H), F32)
    zconv = jnp.zeros((depth, Bp, CONV_WIDTH - 1, W), x_prompt.dtype)
    shared = (layers, conv_w, conv_b, conv_ln_g, conv_ln_b, norm_g, final_g)
    y_p, (pk, pv, pC, pn, pm, pS, pconv) = _trunk(
        x_prompt, 0, None, None, zC, zn, zm, zC, zconv, *shared, _config(Tp))
    y_s, (sk, sv, sC, sn, sm, sS, sconv) = _trunk(
        x_sample, cache_sb_k.shape[2], cache_sb_k, cache_sb_v, state_mlstm_C, state_mlstm_n,
        state_mlstm_m, state_ret_S, state_conv, *shared, _config(x_sample.shape[1]))
    return (y_p, y_s, pk, pv, pC, pn, pm, pS, pconv, sk, sv, sC, sn, sm, sS, sconv)
```

```python
import functools

import numpy as np
import jax
import jax.numpy as jnp
from jax import lax
from jax.experimental import pallas as pl
from jax.experimental.pallas import tpu as pltpu

F32 = jnp.float32
BF16 = jnp.bfloat16

HEAD_DIM = 128
N_BRANCH = 4
CONV_WIDTH = 31
ROPE_BASE = 10000.0
NORM_EPS = 1e-6
LN_EPS = 1e-5
LANES = 128
SUBLANES = 8
HALO = 32
NEG = -1e30
SB_CUT = 104.0
SB_CACHE_GROUP = 4
VMEM_LIMIT = 56 * 1024 * 1024

QA, KA, VA, ZA, QB, KB, VB, OB, ZB, QC, KC, VC, ZC, UD, GD, ZD = range(16)


def _cparams(n_axes):
    return pltpu.CompilerParams(dimension_semantics=("arbitrary",) * n_axes,
                                vmem_limit_bytes=VMEM_LIMIT)


def _dot(a, b):
    return jnp.dot(a, b, preferred_element_type=F32)


def _dot_nt(a, b):
    return lax.dot_general(a, b, (((1,), (1,)), ((), ())), preferred_element_type=F32)


def _dot_tn(a, b):
    return lax.dot_general(a, b, (((0,), (0,)), ((), ())), preferred_element_type=F32)


def _softplus(x):
    return jnp.maximum(x, 0.0) + jnp.log1p(jnp.exp(-jnp.abs(x)))


def _softplus_plain_log(x):
    return jnp.maximum(x, 0.0) + jnp.log(1.0 + jnp.exp(-jnp.abs(x)))


def _log_sigmoid(x):
    return -_softplus(-x)


def _sigmoid(x):
    return 1.0 / (1.0 + jnp.exp(-x))


def _silu(x):
    return x * _sigmoid(x)


def _head_norm(x):
    mu = jnp.mean(x, axis=-1, keepdims=True)
    xc = x - mu
    return xc * lax.rsqrt(jnp.mean(xc * xc, axis=-1, keepdims=True) + LN_EPS)


def _head_norm_mxu(x):
    j = jnp.full((HEAD_DIM, HEAD_DIM), 1.0 / HEAD_DIM, BF16)
    xc = x - _split_dot(x, j, 2)
    return xc * lax.rsqrt(_split_dot(xc * xc, j, 2) + LN_EPS)


def _run_staged(stage_iters, lead):
    live = list(stage_iters)
    for it, n in zip(live, lead):
        for _ in range(n):
            next(it, None)
    while live:
        for it in list(live):
            if next(it, StopIteration) is StopIteration:
                live.remove(it)


def _split_dot(x, m, terms):
    out = None
    r = x
    for t in range(terms):
        p = r.astype(BF16)
        d = _dot(p, m)
        out = d if out is None else out + d
        if t + 1 < terms:
            r = r - p.astype(F32)
    return out


def _pad_rows(x, rows):
    if x.shape[0] == rows:
        return x
    return jnp.concatenate([x, jnp.zeros((rows - x.shape[0],) + x.shape[1:], x.dtype)], axis=0)


def _inproj_kernel(x_ref, g_ref, w_ref, wif_ref, kprev_ref, vprev_ref, pg_ref, k_ref, v_ref, if_ref, hin_ref,
                   *, W, H, rows):
    del kprev_ref, vprev_ref
    j = pl.program_id(1)
    tm = x_ref.shape[0]

    @pl.when(j == 0)
    def _():
        for r in range(0, tm, rows):
            x = x_ref[r:r + rows, :]
            ms = jnp.mean(x * x, axis=-1, keepdims=True)
            hb = ((x * lax.rsqrt(ms + NORM_EPS)) * g_ref[...]).astype(BF16)
            hin_ref[r:r + rows, :] = hb
            if_ref[r:r + rows, :] = _dot_nt(hb, wif_ref[...])

    acc = _dot_nt(hin_ref[...], w_ref[...])
    pg_ref[...] = acc.astype(BF16)

    def put_state(ref, col0):
        for h in range(H):
            ref[pl.ds(h, tm, stride=H), :] = acc[:, col0 + h * HEAD_DIM:col0 + (h + 1) * HEAD_DIM]

    @pl.when(j == 0)
    def _():
        put_state(k_ref, KA * W)

    @pl.when(j == 1)
    def _():
        put_state(v_ref, (VA - 2) * W)


def _inproj(x2, norm_g, wt_main, wt_if, kv_prev, layer, depth, H, tm):
    N, D = x2.shape
    NC = wt_main.shape[0]
    W = D // N_BRANCH
    tn = 2 * W
    rows = min(tm, 256)
    kern = functools.partial(_inproj_kernel, W=W, H=H, rows=rows)
    kv_shape = jax.ShapeDtypeStruct((depth, N * H, HEAD_DIM), F32)
    in_specs = [
        pl.BlockSpec((tm, D), lambda i, j: (i, 0)),
        pl.BlockSpec((1, D), lambda i, j: (0, 0)),
        pl.BlockSpec((tn, D), lambda i, j: (j, 0)),
        pl.BlockSpec((LANES, D), lambda i, j: (0, 0)),
    ]
    if kv_prev is None:
        kv_prev = (jnp.zeros(kv_shape.shape, F32), jnp.zeros(kv_shape.shape, F32))
    in_specs += [pl.BlockSpec(memory_space=pl.ANY)] * 2
    aliases = {4: 1, 5: 2}
    kv_spec = pl.BlockSpec((None, tm * H, HEAD_DIM), lambda i, j: (layer, i, 0))
    return pl.pallas_call(
        kern,
        grid=(N // tm, NC // tn),
        in_specs=in_specs,
        out_specs=[
            pl.BlockSpec((tm, tn), lambda i, j: (i, j)),
            kv_spec,
            kv_spec,
            pl.BlockSpec((tm, LANES), lambda i, j: (i, 0)),
        ],
        out_shape=[jax.ShapeDtypeStruct((N, NC), BF16), kv_shape, kv_shape,
                   jax.ShapeDtypeStruct((N, LANES), F32)],
        scratch_shapes=[pltpu.VMEM((tm, D), BF16)],
        input_output_aliases=aliases,
        compiler_params=_cparams(2),
        name="inproj",
    )(x2, norm_g.reshape(1, D), wt_main, wt_if, *kv_prev)


SB_SCALE = HEAD_DIM ** -0.5


def _strict_upper(n):
    return (lax.broadcasted_iota(jnp.int32, (n, n), 0)
            > lax.broadcasted_iota(jnp.int32, (n, n), 1)).astype(BF16)


def _sb_new_rows(q_ref, kn_ref, vn_ref, acc_ref, car_ref, units, tq):
    tkd = max(tq, LANES)
    u_new = _strict_upper(tkd)
    causal = (lax.broadcasted_iota(jnp.int32, (tq, tkd), 1)
              < lax.broadcasted_iota(jnp.int32, (tq, tkd), 0))
    zs = [_dot_nt(q_ref[g, :, sl], _pad_rows(kn_ref[g, :, sl], tkd)) * SB_SCALE for g, sl in units]
    sps = [jnp.where(causal, _softplus_plain_log(z), 0.0) for z in zs]
    cums = [_split_dot(sp, u_new, 2) for sp in sps]
    ws = [jnp.where(causal, jnp.exp(z - sp - cum), 0.0).astype(BF16) for z, sp, cum in zip(zs, sps, cums)]
    for u, (g, sl) in enumerate(units):
        acc_ref[g, :, sl] = _dot(ws[u], _pad_rows(vn_ref[g, :, sl], tkd))
        car_ref[u] = jnp.sum(sps[u], axis=1, keepdims=True)


def _sb_past_block(q_ref, k_of, v_of, acc_ref, car_ref, units, u_past):
    zb = [_dot_nt(q_ref[g, :, sl], k_of(u)) * SB_SCALE for u, (g, sl) in enumerate(units)]
    spb = [_softplus_plain_log(z) for z in zb]
    cumb = [_split_dot(sp, u_past, 2) for sp in spb]
    cars = [car_ref[u] for u in range(len(units))]
    wb = [jnp.exp(z - sp - cum - car).astype(BF16) for z, sp, cum, car in zip(zb, spb, cumb, cars)]
    lowest = None
    for u, (g, sl) in enumerate(units):
        acc_ref[g, :, sl] += _dot(wb[u], v_of(u))
        car_new = cars[u] + jnp.sum(spb[u], axis=1, keepdims=True)
        car_ref[u] = car_new
        lowest = car_new if lowest is None else jnp.minimum(lowest, car_new)
    return jnp.min(lowest)


def _sb_units(G, H):
    return [(g, slice(h * HEAD_DIM, (h + 1) * HEAD_DIM)) for g in range(G) for h in range(H)]


def _sb_kernel(q_ref, kn_ref, vn_ref, z_ref, kp_ref, vp_ref, o_ref, acc_ref, car_ref, *, tq, tk, H):
    qi = pl.program_id(1)
    units = _sb_units(1, H)
    _sb_new_rows(q_ref, kn_ref, vn_ref, acc_ref, car_ref, units, tq)
    u_past = _strict_upper(tk)

    def cond(c):
        kb, go = c
        return jnp.logical_and(kb >= 0, go)

    def body(c):
        kb, _ = c
        start = pl.multiple_of(kb * tk, tk)
        lowest = _sb_past_block(q_ref, lambda u: kp_ref[pl.ds(start, tk), units[u][1]],
                                lambda u: vp_ref[pl.ds(start, tk), units[u][1]], acc_ref, car_ref, units, u_past)
        return kb - 1, lowest < SB_CUT

    lax.while_loop(cond, body, ((qi * tq) // tk - 1, True))
    o_ref[...] = (acc_ref[...] * _silu(z_ref[...].astype(F32))).astype(BF16)


def _sb_cache_kernel(q_ref, kn_ref, vn_ref, z_ref, kc_ref, vc_ref, o_ref, acc_ref, car_ref, kbuf, vbuf, sem,
                     *, tq, tk, G, H, layer, n_blocks):
    b0 = pl.program_id(0) * G
    units = _sb_units(G, H)
    rows = tk * H

    def copies(i):
        src = pl.ds((n_blocks - 1 - i) * rows, rows)
        slot = i % 2
        out = []
        for g in range(G):
            out.append(pltpu.make_async_copy(kc_ref.at[layer, b0 + g, src, :], kbuf.at[slot, g], sem.at[0, slot, g]))
            out.append(pltpu.make_async_copy(vc_ref.at[layer, b0 + g, src, :], vbuf.at[slot, g], sem.at[1, slot, g]))
        return out

    for c in copies(0):
        c.start()
    _sb_new_rows(q_ref, kn_ref, vn_ref, acc_ref, car_ref, units, tq)
    u_past = _strict_upper(tk)

    def cond(c):
        i, go = c
        return jnp.logical_and(i < n_blocks, go)

    def body(c):
        i, _ = c
        slot = i % 2
        for cp in copies(i):
            cp.wait()

        @pl.when(i + 1 < n_blocks)
        def _():
            for cp in copies(i + 1):
                cp.start()

        head_rows = lambda buf, u: buf[slot, u // H, pl.ds(u % H, tk, stride=H), :].astype(BF16)
        lowest = _sb_past_block(q_ref, lambda u: head_rows(kbuf, u), lambda u: head_rows(vbuf, u),
                                acc_ref, car_ref, units, u_past)
        return i + 1, lowest < SB_CUT

    done, _ = lax.while_loop(cond, body, (0, True))

    @pl.when(done < n_blocks)
    def _():
        for cp in copies(done):
            cp.wait()

    o_ref[...] = (acc_ref[...] * _silu(z_ref[...].astype(F32))).astype(BF16)


def _stick_breaking(pg3, cache_k, cache_v, layer, H, tq, tk):
    B, T, NC = pg3.shape
    W = H * HEAD_DIM
    if cache_k is None:
        G = 1
        kp, vp = pg3, pg3
        kp_spec = pl.BlockSpec((None, T, W), lambda b, i: (b, 0, KA), pipeline_mode=pl.Buffered(1))
        vp_spec = pl.BlockSpec((None, T, W), lambda b, i: (b, 0, VA), pipeline_mode=pl.Buffered(1))
        kern = functools.partial(_sb_kernel, tq=tq, tk=tk, H=H)
        extra = []
    else:
        assert T == tq and cache_k.shape[2] % (tk * H) == 0 and cache_k.shape[2] > 0
        G = SB_CACHE_GROUP if B % SB_CACHE_GROUP == 0 else 1
        kp, vp = cache_k, cache_v
        kp_spec = vp_spec = pl.BlockSpec(memory_space=pl.ANY)
        kern = functools.partial(_sb_cache_kernel, tq=tq, tk=tk, G=G, H=H, layer=layer,
                                 n_blocks=cache_k.shape[2] // (tk * H))
        extra = [pltpu.VMEM((2, G, tk * H, HEAD_DIM), F32), pltpu.VMEM((2, G, tk * H, HEAD_DIM), F32),
                 pltpu.SemaphoreType.DMA((2, 2, G))]
    blk = lambda g: pl.BlockSpec((G, tq, W), lambda b, i, g=g: (b, i, g))
    return pl.pallas_call(
        kern,
        grid=(B // G, T // tq),
        in_specs=[blk(QA), blk(KA), blk(VA), blk(ZA), kp_spec, vp_spec],
        out_specs=pl.BlockSpec((G, tq, W), lambda b, i: (b, i, 0)),
        out_shape=jax.ShapeDtypeStruct((B, T, W), BF16),
        scratch_shapes=[pltpu.VMEM((G, tq, W), F32), pltpu.VMEM((G * H, tq, 1), F32)] + extra,
        compiler_params=_cparams(2),
        name="stick_breaking",
    )(pg3, pg3, pg3, pg3, kp, vp)


def _mlstm_stages(q_ref, k_ref, v_ref, o_ref, z_ref, if_ref, bias_ref, c0_ref, n0_ref, m0_ref,
                  y_ref, c_ref, n_ref, m_ref, *, L, Lc, H):
    ci = pl.program_id(1)
    scale = HEAD_DIM ** -0.5

    @pl.when(ci == 0)
    def _():
        c_ref[...] = c0_ref[...]
        n_ref[...] = n0_ref[...]
        m_ref[...] = m0_ref[...]

    yield
    ifv = _pad_rows(if_ref[...] + bias_ref[...], Lc)
    valid = lax.broadcasted_iota(jnp.int32, (Lc, LANES), 0) < L
    ipre = jnp.where(valid, ifv, NEG)
    logf = jnp.where(valid, _log_sigmoid(ifv), 0.0)
    tri = (lax.broadcasted_iota(jnp.int32, (Lc, Lc), 1)
           <= lax.broadcasted_iota(jnp.int32, (Lc, Lc), 0))
    b_all = _split_dot_left(tri.astype(BF16), logf, 3)
    c_all = ipre - pltpu.roll(b_all, LANES - H, axis=1)
    c_all_t = c_all.T

    hs = range(H)
    heads = [slice(h * HEAD_DIM, (h + 1) * HEAD_DIM) for h in hs]
    qh = [_pad_rows(q_ref[:, sl], Lc) for sl in heads]
    kh = [_pad_rows(k_ref[:, sl], Lc) for sl in heads]
    vh = [_pad_rows(v_ref[:, sl], Lc) for sl in heads]
    C = [c_ref[h] for h in hs]
    n_row = [n_ref[h:h + 1, :] for h in hs]
    m_prev = [m_ref[h:h + 1, 0:1] for h in hs]
    s_qk = [_dot_nt(qh[h], kh[h]) * scale for h in hs]
    q_c = [_dot(qh[h], C[h].astype(BF16)) for h in hs]

    yield
    bcol = [b_all[:, H + h:H + h + 1] for h in hs]
    b_last = [bcol[h][Lc - 1:Lc, :] for h in hs]
    gcol = [b_last[h] + c_all[:, h:h + 1] for h in hs]
    m_new = [jnp.maximum(m_prev[h] + b_last[h], jnp.max(gcol[h], axis=0, keepdims=True)) for h in hs]
    decay = [jnp.exp(m_prev[h] + b_last[h] - m_new[h]) for h in hs]
    kw = [kh[h].astype(F32) * (scale * jnp.exp(gcol[h] - m_new[h])) for h in hs]
    kw_v = [_dot_tn(kw[h].astype(BF16), vh[h]) for h in hs]
    for h in hs:
        c_ref[h] = decay[h] * C[h] + kw_v[h]
        n_ref[h:h + 1, :] = decay[h] * n_row[h] + jnp.sum(kw[h], axis=0, keepdims=True)
        m_ref[h:h + 1, :] = jnp.broadcast_to(m_new[h], (1, LANES))

    yield
    inter = [m_prev[h] + bcol[h] for h in hs]
    m_row, w, w_v = [], [], []
    for h in hs:
        logD = jnp.where(tri, bcol[h] + c_all_t[h:h + 1, :], NEG)
        m_row.append(jnp.maximum(inter[h], jnp.max(logD, axis=1, keepdims=True)))
        w.append(jnp.exp(logD - m_row[h]) * s_qk[h])
        w_v.append(_dot(w[h].astype(BF16), vh[h]))
        yield
    hh = []
    for h in hs:
        a_inter = jnp.exp(inter[h] - m_row[h])
        num = w_v[h] + a_inter * q_c[h]
        qn = jnp.sum(qh[h].astype(F32) * n_row[h], axis=1, keepdims=True)
        den = jnp.sum(w[h], axis=1, keepdims=True) + a_inter * qn
        hh.append(num / jnp.maximum(jnp.abs(den), jnp.exp(-m_row[h])))
        yield
    hn = [_head_norm_mxu(x) for x in hh]
    for h, sl in enumerate(heads):
        og = _sigmoid(o_ref[:, sl].astype(F32))
        zg = _silu(z_ref[:, sl].astype(F32))
        y_ref[:, sl] = ((hn[h][:L] * og) * zg).astype(BF16)
        yield


def _split_dot_left(m, x, terms):
    out = None
    r = x
    for t in range(terms):
        p = r.astype(BF16)
        d = _dot(m, p)
        out = d if out is None else out + d
        if t + 1 < terms:
            r = r - p.astype(F32)
    return out


class _Seq:
    def __init__(self, item):
        self.item = item


def _unmark(items):
    return [i.item if isinstance(i, _Seq) else i for i in items]


class _Branch:
    def __init__(self, args, in_specs, out_specs, out_shapes, scratch, stages, lead=0):
        self.args, self.in_specs, self.out_specs = args, in_specs, out_specs
        self.out_shapes, self.scratch, self.stages = out_shapes, scratch, stages
        self.lead = lead


def _mlstm_branch(pg3, if3, bias_if, C0, n0, m0, L, G):
    B, T, NC = pg3.shape
    H = C0.shape[1]
    W = H * HEAD_DIM
    Lc = max(L, LANES)
    blk = lambda g: _Seq(pl.BlockSpec((G, L, W), lambda b, c, g=g: (b, c, g)))
    st4 = _Seq(pl.BlockSpec((G, H, HEAD_DIM, HEAD_DIM), lambda b, c: (b, 0, 0, 0)))
    st3 = _Seq(pl.BlockSpec((G, H, LANES), lambda b, c: (b, 0, 0)))
    return _Branch(
        args=[pg3, pg3, pg3, pg3, pg3, if3, bias_if, C0, n0, m0],
        in_specs=[blk(QB), blk(KB), blk(VB), blk(OB), blk(ZB),
                  _Seq(pl.BlockSpec((G, L, LANES), lambda b, c: (b, c, 0))),
                  pl.BlockSpec((1, LANES), lambda b, c: (0, 0)),
                  st4, st3, st3],
        out_specs=[_Seq(pl.BlockSpec((G, L, W), lambda b, c: (b, c, 0))), st4, st3, st3],
        out_shapes=[jax.ShapeDtypeStruct((B, T, W), BF16),
                    jax.ShapeDtypeStruct((B, H, HEAD_DIM, HEAD_DIM), F32),
                    jax.ShapeDtypeStruct((B, H, LANES), F32),
                    jax.ShapeDtypeStruct((B, H, LANES), F32)],
        scratch=[],
        stages=functools.partial(_mlstm_stages, L=L, Lc=Lc, H=H))


def _ret_stages(q_ref, k_ref, v_ref, z_ref, cos_ref, sin_ref, dm_ref, qd_ref, kd_ref, s0_ref,
                y_ref, s_ref, *, L, Lc, H, chunk_decay):
    ci = pl.program_id(1)
    scale = HEAD_DIM ** -0.5

    @pl.when(ci == 0)
    def _():
        s_ref[...] = s0_ref[...]

    yield
    cos2 = _pad_rows(cos_ref[...], Lc)
    sin2 = _pad_rows(sin_ref[...], Lc)

    def rot(x):
        return x * cos2 + pltpu.roll(x, HEAD_DIM // 2, axis=1) * sin2

    hs = range(H)
    heads = [slice(h * HEAD_DIM, (h + 1) * HEAD_DIM) for h in hs]
    qb = [rot(_pad_rows(q_ref[:, sl], Lc).astype(F32)).astype(BF16) for sl in heads]
    kr = [rot(_pad_rows(k_ref[:, sl], Lc).astype(F32)) * scale for sl in heads]
    vh = [_pad_rows(v_ref[:, sl], Lc) for sl in heads]
    S = [s_ref[h] for h in hs]
    qk = [_dot_nt(qb[h], kr[h].astype(BF16)) for h in hs]
    q_s = [_dot(qb[h], S[h].astype(BF16)) for h in hs]
    yield
    kd_v = [_dot_tn((kr[h] * kd_ref[:, h:h + 1]).astype(BF16), vh[h]) for h in hs]
    for h in hs:
        s_ref[h] = chunk_decay[h] * S[h] + kd_v[h]
    yield
    att_v = [_dot((qk[h] * dm_ref[h]).astype(BF16), vh[h]) for h in hs]
    yield
    hn = [_head_norm_mxu(att_v[h] + q_s[h] * qd_ref[:, h:h + 1]) for h in hs]
    for h, sl in enumerate(heads):
        zg = _silu(z_ref[:, sl].astype(F32))
        y_ref[:, sl] = (hn[h][:L] * zg).astype(BF16)


def _retention_tables(H, L, Lc):
    lg = np.log1p(-np.exp2(-5.0 - np.arange(H, dtype=np.float64)))
    t = np.arange(Lc, dtype=np.float64)
    rel = t[:, None] - t[None, :]
    dmask = np.where(rel >= 0, np.exp(lg[:, None, None] * np.maximum(rel, 0.0)), 0.0)
    qd = np.zeros((Lc, LANES))
    kd = np.zeros((Lc, LANES))
    qd[:, :H] = np.exp(lg[None, :] * (t[:, None] + 1.0))
    kd[:, :H] = np.where(t[:, None] < L, np.exp(lg[None, :] * (L - 1.0 - t[:, None])), 0.0)
    chunk_decay = tuple(float(v) for v in np.exp(lg * L))
    return (jnp.asarray(dmask, F32), jnp.asarray(qd, F32), jnp.asarray(kd, F32), chunk_decay)


def _retention_branch(pg3, cos2, sin2, S0, L, G):
    B, T, NC = pg3.shape
    H = S0.shape[1]
    W = H * HEAD_DIM
    Lc = max(L, LANES)
    dmask, qd, kd, chunk_decay = _retention_tables(H, L, Lc)
    blk = lambda g: _Seq(pl.BlockSpec((G, L, W), lambda b, c, g=g: (b, c, g)))
    st4 = _Seq(pl.BlockSpec((G, H, HEAD_DIM, HEAD_DIM), lambda b, c: (b, 0, 0, 0)))
    tab = pl.BlockSpec((L, HEAD_DIM), lambda b, c: (c, 0))
    return _Branch(
        args=[pg3, pg3, pg3, pg3, cos2, sin2, dmask, qd, kd, S0],
        in_specs=[blk(QC), blk(KC), blk(VC), blk(ZC), tab, tab,
                  pl.BlockSpec((H, Lc, Lc), lambda b, c: (0, 0, 0)),
                  pl.BlockSpec((Lc, LANES), lambda b, c: (0, 0)),
                  pl.BlockSpec((Lc, LANES), lambda b, c: (0, 0)),
                  st4],
        out_specs=[_Seq(pl.BlockSpec((G, L, W), lambda b, c: (b, c, 0))), st4],
        out_shapes=[jax.ShapeDtypeStruct((B, T, W), BF16),
                    jax.ShapeDtypeStruct((B, H, HEAD_DIM, HEAD_DIM), F32)],
        scratch=[],
        stages=functools.partial(_ret_stages, L=L, Lc=Lc, H=H, chunk_decay=chunk_decay))


def _conv_stages(u_ref, g_ref, z_ref, c0_ref, w_ref, b_ref, lg_ref, lb_ref, y_ref, c1_ref, xs_ref, cv_ref,
                 sh_ref, *, tt, rows):
    ti = pl.program_id(1)
    hist = CONV_WIDTH - 1
    off = HALO - hist
    W = xs_ref.shape[1]

    @pl.when(ti == 0)
    def _():
        xs_ref[0:HALO, :] = jnp.zeros((HALO, W), F32)
        xs_ref[off:HALO, :] = c0_ref[...]

    @pl.when(ti > 0)
    def _():
        xs_ref[0:HALO, :] = xs_ref[tt:tt + HALO, :]

    yield
    for r in range(0, tt, rows):
        xs_ref[HALO + r:HALO + r + rows, :] = (u_ref[r:r + rows, :].astype(F32)
                                               * _sigmoid(g_ref[r:r + rows, :].astype(F32)))
    c1_ref[...] = xs_ref[tt + off:tt + HALO, :]

    for s in range(SUBLANES):
        span = tt + (CONV_WIDTH - 1 - s) // SUBLANES * SUBLANES
        sh_ref[s, 0:span, :] = xs_ref[off + s:off + s + span, :]

    yield
    for r in range(0, tt, rows):
        for c in range(0, W, LANES):
            acc = jnp.broadcast_to(b_ref[:, c:c + LANES], (rows, LANES))
            for j in range(CONV_WIDTH):
                a, s = j // SUBLANES * SUBLANES, j % SUBLANES
                acc = acc + w_ref[j:j + 1, c:c + LANES] * sh_ref[s, r + a:r + a + rows, c:c + LANES]
            cv_ref[r:r + rows, c:c + LANES] = acc
            yield
        ln = _head_norm(cv_ref[r:r + rows, :]) * lg_ref[...] + lb_ref[...]
        y_ref[r:r + rows, :] = (_silu(ln) * _silu(z_ref[r:r + rows, :].astype(F32))).astype(BF16)
        yield


def _conv_branch(pg3, conv0, conv_w, conv_b, ln_g, ln_b, tt, G):
    B, T, NC = pg3.shape
    W = conv0.shape[-1]
    hist = CONV_WIDTH - 1
    rows = min(tt, 64)
    blk = lambda g: _Seq(pl.BlockSpec((G, tt, W), lambda b, t, g=g: (b, t, g)))
    vec = pl.BlockSpec((1, W), lambda b, t: (0, 0))
    return _Branch(
        args=[pg3, pg3, pg3, conv0, conv_w, conv_b.reshape(1, W), ln_g.reshape(1, W), ln_b.reshape(1, W)],
        in_specs=[blk(UD), blk(GD), blk(ZD),
                  _Seq(pl.BlockSpec((G, hist, W), lambda b, t: (b, 0, 0))),
                  pl.BlockSpec((CONV_WIDTH, W), lambda b, t: (0, 0)),
                  vec, vec, vec],
        out_specs=[_Seq(pl.BlockSpec((G, tt, W), lambda b, t: (b, t, 0))),
                   _Seq(pl.BlockSpec((G, hist, W), lambda b, t: (b, 0, 0)))],
        out_shapes=[jax.ShapeDtypeStruct((B, T, W), BF16),
                    jax.ShapeDtypeStruct((B, hist, W), F32)],
        scratch=[_Seq(pltpu.VMEM((G, tt + HALO, W), F32)), _Seq(pltpu.VMEM((G, tt, W), F32)),
                 _Seq(pltpu.VMEM((G, SUBLANES, tt + HALO - SUBLANES, W), F32))],
        stages=functools.partial(_conv_stages, tt=tt, rows=rows),
        lead=2)


def _mixers_kernel(*refs, branches, G):
    n_in = sum(len(b.in_specs) for b in branches)
    n_out = sum(len(b.out_specs) for b in branches)
    pools = [list(refs[:n_in]), list(refs[n_in:n_in + n_out]), list(refs[n_in + n_out:])]
    iters, lead = [], []
    for b in branches:
        marked = [b.in_specs, b.out_specs, b.scratch]
        mine = [[pool.pop(0) for _ in group] for pool, group in zip(pools, marked)]
        for g in range(G):
            view = [r.at[g] if isinstance(m, _Seq) else r
                    for group, ms in zip(mine, marked) for r, m in zip(group, ms)]
            iters.append(b.stages(*view))
            lead.append(b.lead)
    _run_staged(iters, lead)


def _mixers(branches, grid, G):
    outs = pl.pallas_call(
        functools.partial(_mixers_kernel, branches=branches, G=G),
        grid=grid,
        in_specs=_unmark([s for b in branches for s in b.in_specs]),
        out_specs=_unmark([s for b in branches for s in b.out_specs]),
        out_shape=[s for b in branches for s in b.out_shapes],
        scratch_shapes=_unmark([s for b in branches for s in b.scratch]),
        compiler_params=_cparams(len(grid)),
        name="mixers",
    )(*[a for b in branches for a in b.args])
    split, k = [], 0
    for b in branches:
        split.append(outs[k:k + len(b.out_specs)])
        k += len(b.out_specs)
    return split


def _merge_kernel(ya_ref, yb_ref, yc_ref, yd_ref, g_ref, x_ref, wb_ref, wo_ref, fg_ref, o_ref, mg_ref,
                  *, D, W, final):
    ys = (ya_ref, yb_ref, yc_ref, yd_ref)
    for c in range(0, D, W):
        acc = None
        for i in range(N_BRANCH):
            p = _dot(ys[i][...], wb_ref[i, :, c:c + W])
            t = _sigmoid(g_ref[:, i * D + c:i * D + c + W].astype(F32)) * p
            acc = t if acc is None else acc + t
        mg_ref[:, c:c + W] = acc.astype(BF16)
    o = x_ref[...] + _dot(mg_ref[...], wo_ref[...])
    if final:
        ms = jnp.mean(o * o, axis=-1, keepdims=True)
        o = (o * lax.rsqrt(ms + NORM_EPS)) * fg_ref[...]
    o_ref[...] = o


def _merge(ys, pg2, x2, wb, wo, final_g, tm, final):
    N, D = x2.shape
    W = D // N_BRANCH
    yblk = pl.BlockSpec((tm, W), lambda i: (i, 0))
    kern = functools.partial(_merge_kernel, D=D, W=W, final=final)
    return pl.pallas_call(
        kern,
        grid=(N // tm,),
        in_specs=[yblk, yblk, yblk, yblk,
                  pl.BlockSpec((tm, N_BRANCH * D), lambda i: (i, 1)),
                  pl.BlockSpec((tm, D), lambda i: (i, 0)),
                  pl.BlockSpec((N_BRANCH, W, D), lambda i: (0, 0, 0)),
                  pl.BlockSpec((D, D), lambda i: (0, 0)),
                  pl.BlockSpec((1, D), lambda i: (0, 0))],
        out_specs=pl.BlockSpec((tm, D), lambda i: (i, 0)),
        out_shape=jax.ShapeDtypeStruct((N, D), F32),
        scratch_shapes=[pltpu.VMEM((tm, D), BF16)],
        compiler_params=_cparams(1),
        name="merge",
    )(*ys, pg2, x2, wb, wo, final_g.reshape(1, D))


def _rope_tables(pos):
    half = HEAD_DIM // 2
    inv = ROPE_BASE ** (-jnp.arange(half, dtype=F32) / half)
    ang = pos.astype(F32)[:, None] * inv[None, :]
    cos, sin = jnp.cos(ang), jnp.sin(ang)
    return jnp.concatenate([cos, cos], axis=1), jnp.concatenate([-sin, sin], axis=1)


def _wprep_kernel(a_ref, b_ref, g_ref, o_ref, og_ref, *, n_gate, plain_blocks, plain_rows, rows):
    j = pl.program_id(0)
    tn, D = a_ref.shape

    def copy(first_shifted):
        for r in range(0, tn, rows):
            if r < first_shifted:
                v = a_ref[r:r + rows, :]
            elif r + rows + n_gate <= tn:
                v = a_ref[r + n_gate:r + rows + n_gate, :]
            else:
                v = jnp.concatenate([a_ref[r + n_gate:tn, :], b_ref[...]], axis=0)
            o_ref[r:r + rows, :] = v.astype(BF16)

    @pl.when(j < plain_blocks)
    def _():
        copy(tn)

    @pl.when(j == plain_blocks)
    def _():
        copy(plain_rows)

    @pl.when(j > plain_blocks)
    def _():
        copy(0)

    @pl.when(j == 0)
    def _():
        og_ref[...] = jnp.concatenate([g_ref[...], jnp.zeros((LANES - n_gate, D), F32)], axis=0).astype(BF16)


def _wprep(wt_in, layer, H):
    _, n_in, D = wt_in.shape
    W = D // N_BRANCH
    n_gate = 2 * H
    assert n_gate == SUBLANES
    s = (ZB + 1) * W
    NC = n_in - n_gate
    tn = 2 * W
    rows = 256
    assert (s % tn) % rows == 0
    kern = functools.partial(_wprep_kernel, n_gate=n_gate, plain_blocks=s // tn, plain_rows=s % tn, rows=rows)
    return pl.pallas_call(
        kern,
        grid=(NC // tn,),
        in_specs=[pl.BlockSpec((None, tn, D), lambda j: (layer, j, 0)),
                  pl.BlockSpec((None, n_gate, D), lambda j: (layer, (j + 1) * (tn // n_gate), 0)),
                  pl.BlockSpec((None, n_gate, D), lambda j: (layer, s // n_gate, 0))],
        out_specs=[pl.BlockSpec((tn, D), lambda j: (j, 0)),
                   pl.BlockSpec((LANES, D), lambda j: (0, 0))],
        out_shape=[jax.ShapeDtypeStruct((NC, D), BF16), jax.ShapeDtypeStruct((LANES, D), BF16)],
        compiler_params=_cparams(1),
        name="wprep",
    )(wt_in, wt_in, wt_in)


def _prep_layer(wt_in, layer, b_i_l, b_f_l, w_branch_l, w_out_l, H):
    w_main, w_if = _wprep(wt_in, layer, H)
    bias_if = jnp.pad(jnp.concatenate([b_i_l, b_f_l]).astype(F32), (0, LANES - 2 * H)).reshape(1, LANES)
    return w_main, w_if, bias_if, w_branch_l.astype(BF16), w_out_l.astype(BF16)


def _trunk(x, past_len, cache_k, cache_v, C0, n0, m0, S0, conv0, layers, conv_w, conv_b, ln_g, ln_b,
           norm_g, final_g, cfg):
    B, T, D = x.shape
    N = B * T
    W = D // N_BRANCH
    H = W // HEAD_DIM
    depth = len(layers)
    cos2, sin2 = _rope_tables(past_len + jnp.arange(T))
    x2 = x.reshape(N, D)
    if cache_k is None:
        ck4 = cv4 = None
    else:
        ck4 = cache_k.reshape(depth, B, -1, HEAD_DIM)
        cv4 = cache_v.reshape(depth, B, -1, HEAD_DIM)
    states = []
    kv = None
    for l in range(depth):
        w_main, w_if, bias_if, wb, wo = layers[l]
        pg, k_all, v_all, ifp = _inproj(x2, norm_g[l], w_main, w_if, kv, l, depth, H, cfg["tm_in"])
        kv = (k_all, v_all)
        pg3 = pg.reshape(B, T, -1)
        ya = _stick_breaking(pg3, ck4, cv4, l, H, cfg["tq"], cfg["tk"])
        m0b = jnp.broadcast_to(m0[l].astype(F32)[..., None], (B, H, LANES))
        L, G = cfg["L"], cfg["G"]
        (yb, C1, n1, m1), (yc, S1), (yd, conv1) = _mixers(
            [_mlstm_branch(pg3, ifp.reshape(B, T, LANES), bias_if, C0[l].astype(F32), n0[l].astype(F32), m0b, L, G),
             _retention_branch(pg3, cos2, sin2, S0[l].astype(F32), L, G),
             _conv_branch(pg3, conv0[l], conv_w[l], conv_b[l], ln_g[l], ln_b[l], L, G)],
            grid=(B // G, T // L), G=G)
        ys = tuple(y.reshape(N, W) for y in (ya, yb, yc, yd))
        x2 = _merge(ys, pg, x2, wb, wo, final_g, cfg["tm_mg"], final=(l == depth - 1))
        states.append((C1, n1, m1[..., 0], S1, conv1))
    stacked = tuple(jnp.stack(s, axis=0) for s in zip(*states))
    kv5 = tuple(a.reshape(depth, B, T, H, HEAD_DIM) for a in kv)
    return x2.reshape(B, T, D), kv5 + stacked


def _config(B, T):
    if T >= 256:
        return dict(tm_in=1024, tq=256, tk=256, L=256, tm_mg=256, G=1)
    return dict(tm_in=512, tq=T, tk=128, L=T, tm_mg=256, G=4 if B % 4 == 0 else 1)


def kernel(x_prompt, x_sample, cache_sb_k, cache_sb_v, state_mlstm_C, state_mlstm_n, state_mlstm_m,
           state_ret_S, state_conv, norm_g, w_in, mlstm_b_i, mlstm_b_f, conv_w, conv_b, conv_ln_g,
           conv_ln_b, w_branch, w_out, final_g):
    depth = w_in.shape[0]
    Bp, Tp, D = x_prompt.shape
    W = D // N_BRANCH
    H = W // HEAD_DIM
    wt_in = jnp.swapaxes(w_in, 1, 2)
    layers = [_prep_layer(wt_in, l, mlstm_b_i[l], mlstm_b_f[l], w_branch[l], w_out[l], H) for l in range(depth)]
    zC = jnp.zeros((depth, Bp, H, HEAD_DIM, HEAD_DIM), F32)
    zn = jnp.zeros((depth, Bp, H, HEAD_DIM), F32)
    zm = jnp.zeros((depth, Bp, H), F32)
    zconv = jnp.zeros((depth, Bp, CONV_WIDTH - 1, W), x_prompt.dtype)
    shared = (layers, conv_w, conv_b, conv_ln_g, conv_ln_b, norm_g, final_g)
    y_p, (pk, pv, pC, pn, pm, pS, pconv) = _trunk(
        x_prompt, 0, None, None, zC, zn, zm, zC, zconv, *shared, _config(Bp, Tp))
    y_s, (sk, sv, sC, sn, sm, sS, sconv) = _trunk(
        x_sample, cache_sb_k.shape[2], cache_sb_k, cache_sb_v, state_mlstm_C, state_mlstm_n,
        state_mlstm_m, state_ret_S, state_conv, *shared, _config(*x_sample.shape[:2]))
    return (y_p, y_s, pk, pv, pC, pn, pm, pS, pconv, sk, sv, sC, sn, sm, sS, sconv)
```

```python
import functools

import numpy as np
import jax
import jax.numpy as jnp
from jax import lax
from jax.experimental import pallas as pl
from jax.experimental.pallas import tpu as pltpu

F32 = jnp.float32
BF16 = jnp.bfloat16

HEAD_DIM = 128
N_BRANCH = 4
CONV_WIDTH = 31
ROPE_BASE = 10000.0
NORM_EPS = 1e-6
LN_EPS = 1e-5
LANES = 128
SUBLANES = 8
HALO = 32
NEG = -1e30
SB_CUT = 104.0
SB_CACHE_GROUP = 4
VMEM_LIMIT = 56 * 1024 * 1024

QA, KA, VA, ZA, QB, KB, VB, OB, ZB, QC, KC, VC, ZC, UD, GD, ZD = range(16)


def _cparams(n_axes):
    return pltpu.CompilerParams(dimension_semantics=("arbitrary",) * n_axes,
                                vmem_limit_bytes=VMEM_LIMIT)


def _dot(a, b):
    return jnp.dot(a, b, preferred_element_type=F32)


def _dot_nt(a, b):
    return lax.dot_general(a, b, (((1,), (1,)), ((), ())), preferred_element_type=F32)


def _dot_tn(a, b):
    return lax.dot_general(a, b, (((0,), (0,)), ((), ())), preferred_element_type=F32)


def _softplus(x):
    return jnp.maximum(x, 0.0) + jnp.log1p(jnp.exp(-jnp.abs(x)))


def _softplus_plain_log(x):
    return jnp.maximum(x, 0.0) + jnp.log(1.0 + jnp.exp(-jnp.abs(x)))


def _log_sigmoid(x):
    return -_softplus(-x)


def _sigmoid(x):
    return 1.0 / (1.0 + jnp.exp(-x))


def _silu(x):
    return x * _sigmoid(x)


def _head_norm(x):
    mu = jnp.mean(x, axis=-1, keepdims=True)
    xc = x - mu
    return xc * lax.rsqrt(jnp.mean(xc * xc, axis=-1, keepdims=True) + LN_EPS)


def _head_norm_mxu(x):
    j = jnp.full((HEAD_DIM, HEAD_DIM), 1.0 / HEAD_DIM, BF16)
    xc = x - _split_dot(x, j, 2)
    return xc * lax.rsqrt(_split_dot(xc * xc, j, 2) + LN_EPS)


def _run_staged(stage_iters, lead):
    live = list(stage_iters)
    for it, n in zip(live, lead):
        for _ in range(n):
            next(it, None)
    while live:
        for it in list(live):
            if next(it, StopIteration) is StopIteration:
                live.remove(it)


def _split_dot(x, m, terms):
    out = None
    r = x
    for t in range(terms):
        p = r.astype(BF16)
        d = _dot(p, m)
        out = d if out is None else out + d
        if t + 1 < terms:
            r = r - p.astype(F32)
    return out


def _pad_rows(x, rows):
    if x.shape[0] == rows:
        return x
    return jnp.concatenate([x, jnp.zeros((rows - x.shape[0],) + x.shape[1:], x.dtype)], axis=0)


def _inproj_kernel(x_ref, g_ref, w_ref, wif_ref, kprev_ref, vprev_ref, pg_ref, k_ref, v_ref, if_ref, hin_ref,
                   *, W, H, rows):
    del kprev_ref, vprev_ref
    j = pl.program_id(1)
    tm = x_ref.shape[0]

    @pl.when(j == 0)
    def _():
        for r in range(0, tm, rows):
            x = x_ref[r:r + rows, :]
            ms = jnp.mean(x * x, axis=-1, keepdims=True)
            hb = ((x * lax.rsqrt(ms + NORM_EPS)) * g_ref[...]).astype(BF16)
            hin_ref[r:r + rows, :] = hb
            if_ref[r:r + rows, :] = _dot_nt(hb, wif_ref[...])

    acc = _dot_nt(hin_ref[...], w_ref[...])
    pg_ref[...] = acc.astype(BF16)

    def put_state(ref, col0):
        for h in range(H):
            ref[pl.ds(h, tm, stride=H), :] = acc[:, col0 + h * HEAD_DIM:col0 + (h + 1) * HEAD_DIM]

    @pl.when(j == 0)
    def _():
        put_state(k_ref, KA * W)

    @pl.when(j == 1)
    def _():
        put_state(v_ref, (VA - 2) * W)


def _inproj(x2, norm_g, wt_main, wt_if, kv_prev, layer, depth, H, tm):
    N, D = x2.shape
    NC = wt_main.shape[0]
    W = D // N_BRANCH
    tn = 2 * W
    rows = min(tm, 256)
    kern = functools.partial(_inproj_kernel, W=W, H=H, rows=rows)
    kv_shape = jax.ShapeDtypeStruct((depth, N * H, HEAD_DIM), F32)
    in_specs = [
        pl.BlockSpec((tm, D), lambda i, j: (i, 0)),
        pl.BlockSpec((1, D), lambda i, j: (0, 0)),
        pl.BlockSpec((tn, D), lambda i, j: (j, 0)),
        pl.BlockSpec((LANES, D), lambda i, j: (0, 0)),
    ]
    if kv_prev is None:
        kv_prev = (jnp.zeros(kv_shape.shape, F32), jnp.zeros(kv_shape.shape, F32))
    in_specs += [pl.BlockSpec(memory_space=pl.ANY)] * 2
    aliases = {4: 1, 5: 2}
    kv_spec = pl.BlockSpec((None, tm * H, HEAD_DIM), lambda i, j: (layer, i, 0))
    return pl.pallas_call(
        kern,
        grid=(N // tm, NC // tn),
        in_specs=in_specs,
        out_specs=[
            pl.BlockSpec((tm, tn), lambda i, j: (i, j)),
            kv_spec,
            kv_spec,
            pl.BlockSpec((tm, LANES), lambda i, j: (i, 0)),
        ],
        out_shape=[jax.ShapeDtypeStruct((N, NC), BF16), kv_shape, kv_shape,
                   jax.ShapeDtypeStruct((N, LANES), F32)],
        scratch_shapes=[pltpu.VMEM((tm, D), BF16)],
        input_output_aliases=aliases,
        compiler_params=_cparams(2),
        name="inproj",
    )(x2, norm_g.reshape(1, D), wt_main, wt_if, *kv_prev)


SB_SCALE = HEAD_DIM ** -0.5


def _strict_upper(n):
    return (lax.broadcasted_iota(jnp.int32, (n, n), 0)
            > lax.broadcasted_iota(jnp.int32, (n, n), 1)).astype(BF16)


def _sb_new_rows(q_ref, kn_ref, vn_ref, acc_ref, car_ref, units, tq):
    tkd = max(tq, LANES)
    u_new = _strict_upper(tkd)
    causal = (lax.broadcasted_iota(jnp.int32, (tq, tkd), 1)
              < lax.broadcasted_iota(jnp.int32, (tq, tkd), 0))
    zs = [_dot_nt(q_ref[g, :, sl], _pad_rows(kn_ref[g, :, sl], tkd)) * SB_SCALE for g, sl in units]
    sps = [jnp.where(causal, _softplus_plain_log(z), 0.0) for z in zs]
    cums = [_split_dot(sp, u_new, 2) for sp in sps]
    ws = [jnp.where(causal, jnp.exp(z - sp - cum), 0.0).astype(BF16) for z, sp, cum in zip(zs, sps, cums)]
    for u, (g, sl) in enumerate(units):
        acc_ref[g, :, sl] = _dot(ws[u], _pad_rows(vn_ref[g, :, sl], tkd))
        car_ref[u] = jnp.sum(sps[u], axis=1, keepdims=True)


def _sb_past_block(q_ref, k_of, v_of, acc_ref, car_ref, units, u_past):
    zb = [_dot_nt(q_ref[g, :, sl], k_of(u)) * SB_SCALE for u, (g, sl) in enumerate(units)]
    spb = [_softplus_plain_log(z) for z in zb]
    cumb = [_split_dot(sp, u_past, 2) for sp in spb]
    cars = [car_ref[u] for u in range(len(units))]
    wb = [jnp.exp(z - sp - cum - car).astype(BF16) for z, sp, cum, car in zip(zb, spb, cumb, cars)]
    lowest = None
    for u, (g, sl) in enumerate(units):
        acc_ref[g, :, sl] += _dot(wb[u], v_of(u))
        car_new = cars[u] + jnp.sum(spb[u], axis=1, keepdims=True)
        car_ref[u] = car_new
        lowest = car_new if lowest is None else jnp.minimum(lowest, car_new)
    return jnp.min(lowest)


def _sb_units(G, H):
    return [(g, slice(h * HEAD_DIM, (h + 1) * HEAD_DIM)) for g in range(G) for h in range(H)]


def _sb_kernel(q_ref, kn_ref, vn_ref, z_ref, kp_ref, vp_ref, o_ref, acc_ref, car_ref, *, tq, tk, H):
    qi = pl.program_id(1)
    units = _sb_units(1, H)
    _sb_new_rows(q_ref, kn_ref, vn_ref, acc_ref, car_ref, units, tq)
    u_past = _strict_upper(tk)

    def cond(c):
        kb, go = c
        return jnp.logical_and(kb >= 0, go)

    def body(c):
        kb, _ = c
        start = pl.multiple_of(kb * tk, tk)
        lowest = _sb_past_block(q_ref, lambda u: kp_ref[pl.ds(start, tk), units[u][1]],
                                lambda u: vp_ref[pl.ds(start, tk), units[u][1]], acc_ref, car_ref, units, u_past)
        return kb - 1, lowest < SB_CUT

    lax.while_loop(cond, body, ((qi * tq) // tk - 1, True))
    o_ref[...] = (acc_ref[...] * _silu(z_ref[...].astype(F32))).astype(BF16)


def _sb_cache_kernel(q_ref, kn_ref, vn_ref, z_ref, kc_ref, vc_ref, o_ref, acc_ref, car_ref, kbuf, vbuf, sem,
                     *, tq, tk, G, H, layer, n_blocks):
    b0 = pl.program_id(0) * G
    units = _sb_units(G, H)
    rows = tk * H

    def copies(i):
        src = pl.ds((n_blocks - 1 - i) * rows, rows)
        slot = i % 2
        out = []
        for g in range(G):
            out.append(pltpu.make_async_copy(kc_ref.at[layer, b0 + g, src, :], kbuf.at[slot, g], sem.at[0, slot, g]))
            out.append(pltpu.make_async_copy(vc_ref.at[layer, b0 + g, src, :], vbuf.at[slot, g], sem.at[1, slot, g]))
        return out

    for c in copies(0):
        c.start()
    _sb_new_rows(q_ref, kn_ref, vn_ref, acc_ref, car_ref, units, tq)
    u_past = _strict_upper(tk)

    def cond(c):
        i, go = c
        return jnp.logical_and(i < n_blocks, go)

    def body(c):
        i, _ = c
        slot = i % 2
        for cp in copies(i):
            cp.wait()

        @pl.when(i + 1 < n_blocks)
        def _():
            for cp in copies(i + 1):
                cp.start()

        head_rows = lambda buf, u: buf[slot, u // H, pl.ds(u % H, tk, stride=H), :].astype(BF16)
        lowest = _sb_past_block(q_ref, lambda u: head_rows(kbuf, u), lambda u: head_rows(vbuf, u),
                                acc_ref, car_ref, units, u_past)
        return i + 1, lowest < SB_CUT

    done, _ = lax.while_loop(cond, body, (0, True))

    @pl.when(done < n_blocks)
    def _():
        for cp in copies(done):
            cp.wait()

    o_ref[...] = (acc_ref[...] * _silu(z_ref[...].astype(F32))).astype(BF16)


def _stick_breaking(pg3, cache_k, cache_v, layer, H, tq, tk):
    B, T, NC = pg3.shape
    W = H * HEAD_DIM
    if cache_k is None:
        G = 1
        kp, vp = pg3, pg3
        kp_spec = pl.BlockSpec((None, T, W), lambda b, i: (b, 0, KA), pipeline_mode=pl.Buffered(1))
        vp_spec = pl.BlockSpec((None, T, W), lambda b, i: (b, 0, VA), pipeline_mode=pl.Buffered(1))
        kern = functools.partial(_sb_kernel, tq=tq, tk=tk, H=H)
        extra = []
    else:
        assert T == tq and cache_k.shape[2] % (tk * H) == 0 and cache_k.shape[2] > 0
        G = SB_CACHE_GROUP if B % SB_CACHE_GROUP == 0 else 1
        kp, vp = cache_k, cache_v
        kp_spec = vp_spec = pl.BlockSpec(memory_space=pl.ANY)
        kern = functools.partial(_sb_cache_kernel, tq=tq, tk=tk, G=G, H=H, layer=layer,
                                 n_blocks=cache_k.shape[2] // (tk * H))
        extra = [pltpu.VMEM((2, G, tk * H, HEAD_DIM), F32), pltpu.VMEM((2, G, tk * H, HEAD_DIM), F32),
                 pltpu.SemaphoreType.DMA((2, 2, G))]
    blk = lambda g: pl.BlockSpec((G, tq, W), lambda b, i, g=g: (b, i, g))
    return pl.pallas_call(
        kern,
        grid=(B // G, T // tq),
        in_specs=[blk(QA), blk(KA), blk(VA), blk(ZA), kp_spec, vp_spec],
        out_specs=pl.BlockSpec((G, tq, W), lambda b, i: (b, i, 0)),
        out_shape=jax.ShapeDtypeStruct((B, T, W), BF16),
        scratch_shapes=[pltpu.VMEM((G, tq, W), F32), pltpu.VMEM((G * H, tq, 1), F32)] + extra,
        compiler_params=_cparams(2),
        name="stick_breaking",
    )(pg3, pg3, pg3, pg3, kp, vp)


def _mlstm_stages(q_ref, k_ref, v_ref, o_ref, z_ref, if_ref, bias_ref, c0_ref, n0_ref, m0_ref,
                  y_ref, c_ref, n_ref, m_ref, *, L, Lc, H):
    ci = pl.program_id(1)
    scale = HEAD_DIM ** -0.5

    @pl.when(ci == 0)
    def _():
        c_ref[...] = c0_ref[...]
        n_ref[...] = n0_ref[...]
        m_ref[...] = m0_ref[...]

    yield
    ifv = _pad_rows(if_ref[...] + bias_ref[...], Lc)
    valid = lax.broadcasted_iota(jnp.int32, (Lc, LANES), 0) < L
    ipre = jnp.where(valid, ifv, NEG)
    logf = jnp.where(valid, _log_sigmoid(ifv), 0.0)
    tri = (lax.broadcasted_iota(jnp.int32, (Lc, Lc), 1)
           <= lax.broadcasted_iota(jnp.int32, (Lc, Lc), 0))
    b_all = _split_dot_left(tri.astype(BF16), logf, 3)
    c_all = ipre - pltpu.roll(b_all, LANES - H, axis=1)
    c_all_t = c_all.T

    hs = range(H)
    heads = [slice(h * HEAD_DIM, (h + 1) * HEAD_DIM) for h in hs]
    qh = [_pad_rows(q_ref[:, sl], Lc) for sl in heads]
    kh = [_pad_rows(k_ref[:, sl], Lc) for sl in heads]
    vh = [_pad_rows(v_ref[:, sl], Lc) for sl in heads]
    C = [c_ref[h] for h in hs]
    n_row = [n_ref[h:h + 1, :] for h in hs]
    m_prev = [m_ref[h:h + 1, 0:1] for h in hs]
    s_qk = [_dot_nt(qh[h], kh[h]) * scale for h in hs]
    q_c = [_dot(qh[h], C[h].astype(BF16)) for h in hs]

    yield
    bcol = [b_all[:, H + h:H + h + 1] for h in hs]
    b_last = [bcol[h][Lc - 1:Lc, :] for h in hs]
    gcol = [b_last[h] + c_all[:, h:h + 1] for h in hs]
    m_new = [jnp.maximum(m_prev[h] + b_last[h], jnp.max(gcol[h], axis=0, keepdims=True)) for h in hs]
    decay = [jnp.exp(m_prev[h] + b_last[h] - m_new[h]) for h in hs]
    kw = [kh[h].astype(F32) * (scale * jnp.exp(gcol[h] - m_new[h])) for h in hs]
    kw_v = [_dot_tn(kw[h].astype(BF16), vh[h]) for h in hs]
    for h in hs:
        c_ref[h] = decay[h] * C[h] + kw_v[h]
        n_ref[h:h + 1, :] = decay[h] * n_row[h] + jnp.sum(kw[h], axis=0, keepdims=True)
        m_ref[h:h + 1, :] = jnp.broadcast_to(m_new[h], (1, LANES))

    yield
    inter = [m_prev[h] + bcol[h] for h in hs]
    m_row, w, w_v = [], [], []
    for h in hs:
        logD = jnp.where(tri, bcol[h] + c_all_t[h:h + 1, :], NEG)
        m_row.append(jnp.maximum(inter[h], jnp.max(logD, axis=1, keepdims=True)))
        w.append(jnp.exp(logD - m_row[h]) * s_qk[h])
        w_v.append(_dot(w[h].astype(BF16), vh[h]))
        yield
    hh = []
    for h in hs:
        a_inter = jnp.exp(inter[h] - m_row[h])
        num = w_v[h] + a_inter * q_c[h]
        qn = jnp.sum(qh[h].astype(F32) * n_row[h], axis=1, keepdims=True)
        den = jnp.sum(w[h], axis=1, keepdims=True) + a_inter * qn
        hh.append(num / jnp.maximum(jnp.abs(den), jnp.exp(-m_row[h])))
        yield
    hn = [_head_norm_mxu(x) for x in hh]
    for h, sl in enumerate(heads):
        og = _sigmoid(o_ref[:, sl].astype(F32))
        zg = _silu(z_ref[:, sl].astype(F32))
        y_ref[:, sl] = ((hn[h][:L] * og) * zg).astype(BF16)
        yield


def _split_dot_left(m, x, terms):
    out = None
    r = x
    for t in range(terms):
        p = r.astype(BF16)
        d = _dot(m, p)
        out = d if out is None else out + d
        if t + 1 < terms:
            r = r - p.astype(F32)
    return out


class _Seq:
    def __init__(self, item):
        self.item = item


def _unmark(items):
    return [i.item if isinstance(i, _Seq) else i for i in items]


class _Branch:
    def __init__(self, args, in_specs, out_specs, out_shapes, scratch, stages, lead=0):
        self.args, self.in_specs, self.out_specs = args, in_specs, out_specs
        self.out_shapes, self.scratch, self.stages = out_shapes, scratch, stages
        self.lead = lead


def _mlstm_branch(pg3, if3, bias_if, C0, n0, m0, layer, L, G):
    B, T, NC = pg3.shape
    H = C0.shape[2]
    W = H * HEAD_DIM
    Lc = max(L, LANES)
    blk = lambda g: _Seq(pl.BlockSpec((G, L, W), lambda b, c, g=g: (b, c, g)))
    st4 = _Seq(pl.BlockSpec((G, H, HEAD_DIM, HEAD_DIM), lambda b, c: (b, 0, 0, 0)))
    st3 = _Seq(pl.BlockSpec((G, H, LANES), lambda b, c: (b, 0, 0)))
    return _Branch(
        args=[pg3, pg3, pg3, pg3, pg3, if3, bias_if, C0, n0, m0],
        in_specs=[blk(QB), blk(KB), blk(VB), blk(OB), blk(ZB),
                  _Seq(pl.BlockSpec((G, L, LANES), lambda b, c: (b, c, 0))),
                  pl.BlockSpec((1, LANES), lambda b, c: (0, 0)),
                  _Seq(pl.BlockSpec((None, G, H, HEAD_DIM, HEAD_DIM), lambda b, c: (layer, b, 0, 0, 0))),
                  _Seq(pl.BlockSpec((None, G, H, LANES), lambda b, c: (layer, b, 0, 0))),
                  st3],
        out_specs=[_Seq(pl.BlockSpec((G, L, W), lambda b, c: (b, c, 0))), st4, st3, st3],
        out_shapes=[jax.ShapeDtypeStruct((B, T, W), BF16),
                    jax.ShapeDtypeStruct((B, H, HEAD_DIM, HEAD_DIM), F32),
                    jax.ShapeDtypeStruct((B, H, LANES), F32),
                    jax.ShapeDtypeStruct((B, H, LANES), F32)],
        scratch=[],
        stages=functools.partial(_mlstm_stages, L=L, Lc=Lc, H=H))


def _ret_stages(q_ref, k_ref, v_ref, z_ref, cos_ref, sin_ref, dm_ref, qd_ref, kd_ref, s0_ref,
                y_ref, s_ref, *, L, Lc, H, chunk_decay):
    ci = pl.program_id(1)
    scale = HEAD_DIM ** -0.5

    @pl.when(ci == 0)
    def _():
        s_ref[...] = s0_ref[...]

    yield
    cos2 = _pad_rows(cos_ref[...], Lc)
    sin2 = _pad_rows(sin_ref[...], Lc)

    def rot(x):
        return x * cos2 + pltpu.roll(x, HEAD_DIM // 2, axis=1) * sin2

    hs = range(H)
    heads = [slice(h * HEAD_DIM, (h + 1) * HEAD_DIM) for h in hs]
    qb = [rot(_pad_rows(q_ref[:, sl], Lc).astype(F32)).astype(BF16) for sl in heads]
    kr = [rot(_pad_rows(k_ref[:, sl], Lc).astype(F32)) * scale for sl in heads]
    vh = [_pad_rows(v_ref[:, sl], Lc) for sl in heads]
    S = [s_ref[h] for h in hs]
    qk = [_dot_nt(qb[h], kr[h].astype(BF16)) for h in hs]
    q_s = [_dot(qb[h], S[h].astype(BF16)) for h in hs]
    yield
    kd_v = [_dot_tn((kr[h] * kd_ref[:, h:h + 1]).astype(BF16), vh[h]) for h in hs]
    for h in hs:
        s_ref[h] = chunk_decay[h] * S[h] + kd_v[h]
    yield
    att_v = [_dot((qk[h] * dm_ref[h]).astype(BF16), vh[h]) for h in hs]
    yield
    hn = [_head_norm_mxu(att_v[h] + q_s[h] * qd_ref[:, h:h + 1]) for h in hs]
    for h, sl in enumerate(heads):
        zg = _silu(z_ref[:, sl].astype(F32))
        y_ref[:, sl] = (hn[h][:L] * zg).astype(BF16)


def _retention_tables(H, L, Lc):
    lg = np.log1p(-np.exp2(-5.0 - np.arange(H, dtype=np.float64)))
    t = np.arange(Lc, dtype=np.float64)
    rel = t[:, None] - t[None, :]
    dmask = np.where(rel >= 0, np.exp(lg[:, None, None] * np.maximum(rel, 0.0)), 0.0)
    qd = np.zeros((Lc, LANES))
    kd = np.zeros((Lc, LANES))
    qd[:, :H] = np.exp(lg[None, :] * (t[:, None] + 1.0))
    kd[:, :H] = np.where(t[:, None] < L, np.exp(lg[None, :] * (L - 1.0 - t[:, None])), 0.0)
    chunk_decay = tuple(float(v) for v in np.exp(lg * L))
    return (jnp.asarray(dmask, F32), jnp.asarray(qd, F32), jnp.asarray(kd, F32), chunk_decay)


def _retention_branch(pg3, cos2, sin2, S0, layer, L, G):
    B, T, NC = pg3.shape
    H = S0.shape[2]
    W = H * HEAD_DIM
    Lc = max(L, LANES)
    dmask, qd, kd, chunk_decay = _retention_tables(H, L, Lc)
    blk = lambda g: _Seq(pl.BlockSpec((G, L, W), lambda b, c, g=g: (b, c, g)))
    st4 = _Seq(pl.BlockSpec((G, H, HEAD_DIM, HEAD_DIM), lambda b, c: (b, 0, 0, 0)))
    tab = pl.BlockSpec((L, HEAD_DIM), lambda b, c: (c, 0))
    return _Branch(
        args=[pg3, pg3, pg3, pg3, cos2, sin2, dmask, qd, kd, S0],
        in_specs=[blk(QC), blk(KC), blk(VC), blk(ZC), tab, tab,
                  pl.BlockSpec((H, Lc, Lc), lambda b, c: (0, 0, 0)),
                  pl.BlockSpec((Lc, LANES), lambda b, c: (0, 0)),
                  pl.BlockSpec((Lc, LANES), lambda b, c: (0, 0)),
                  _Seq(pl.BlockSpec((None, G, H, HEAD_DIM, HEAD_DIM), lambda b, c: (layer, b, 0, 0, 0)))],
        out_specs=[_Seq(pl.BlockSpec((G, L, W), lambda b, c: (b, c, 0))), st4],
        out_shapes=[jax.ShapeDtypeStruct((B, T, W), BF16),
                    jax.ShapeDtypeStruct((B, H, HEAD_DIM, HEAD_DIM), F32)],
        scratch=[],
        stages=functools.partial(_ret_stages, L=L, Lc=Lc, H=H, chunk_decay=chunk_decay))


def _conv_stages(u_ref, g_ref, z_ref, c0_ref, w_ref, b_ref, lg_ref, lb_ref, y_ref, c1_ref, xs_ref, cv_ref,
                 sh_ref, *, tt, rows):
    ti = pl.program_id(1)
    hist = CONV_WIDTH - 1
    off = HALO - hist
    W = xs_ref.shape[1]

    @pl.when(ti == 0)
    def _():
        xs_ref[0:HALO, :] = jnp.zeros((HALO, W), F32)
        xs_ref[off:HALO, :] = c0_ref[...]

    @pl.when(ti > 0)
    def _():
        xs_ref[0:HALO, :] = xs_ref[tt:tt + HALO, :]

    yield
    for r in range(0, tt, rows):
        xs_ref[HALO + r:HALO + r + rows, :] = (u_ref[r:r + rows, :].astype(F32)
                                               * _sigmoid(g_ref[r:r + rows, :].astype(F32)))
    c1_ref[...] = xs_ref[tt + off:tt + HALO, :]

    for s in range(SUBLANES):
        span = tt + (CONV_WIDTH - 1 - s) // SUBLANES * SUBLANES
        sh_ref[s, 0:span, :] = xs_ref[off + s:off + s + span, :]

    yield
    for r in range(0, tt, rows):
        for c in range(0, W, LANES):
            acc = jnp.broadcast_to(b_ref[:, c:c + LANES], (rows, LANES))
            for j in range(CONV_WIDTH):
                a, s = j // SUBLANES * SUBLANES, j % SUBLANES
                acc = acc + w_ref[j:j + 1, c:c + LANES] * sh_ref[s, r + a:r + a + rows, c:c + LANES]
            cv_ref[r:r + rows, c:c + LANES] = acc
            yield
        ln = _head_norm(cv_ref[r:r + rows, :]) * lg_ref[...] + lb_ref[...]
        y_ref[r:r + rows, :] = (_silu(ln) * _silu(z_ref[r:r + rows, :].astype(F32))).astype(BF16)
        yield


def _conv_branch(pg3, conv0, conv_w, conv_b, ln_g, ln_b, tt, G):
    B, T, NC = pg3.shape
    W = conv0.shape[-1]
    hist = CONV_WIDTH - 1
    rows = min(tt, 64)
    blk = lambda g: _Seq(pl.BlockSpec((G, tt, W), lambda b, t, g=g: (b, t, g)))
    vec = pl.BlockSpec((1, W), lambda b, t: (0, 0))
    return _Branch(
        args=[pg3, pg3, pg3, conv0, conv_w, conv_b.reshape(1, W), ln_g.reshape(1, W), ln_b.reshape(1, W)],
        in_specs=[blk(UD), blk(GD), blk(ZD),
                  _Seq(pl.BlockSpec((G, hist, W), lambda b, t: (b, 0, 0))),
                  pl.BlockSpec((CONV_WIDTH, W), lambda b, t: (0, 0)),
                  vec, vec, vec],
        out_specs=[_Seq(pl.BlockSpec((G, tt, W), lambda b, t: (b, t, 0))),
                   _Seq(pl.BlockSpec((G, hist, W), lambda b, t: (b, 0, 0)))],
        out_shapes=[jax.ShapeDtypeStruct((B, T, W), BF16),
                    jax.ShapeDtypeStruct((B, hist, W), F32)],
        scratch=[_Seq(pltpu.VMEM((G, tt + HALO, W), F32)), _Seq(pltpu.VMEM((G, tt, W), F32)),
                 _Seq(pltpu.VMEM((G, SUBLANES, tt + HALO - SUBLANES, W), F32))],
        stages=functools.partial(_conv_stages, tt=tt, rows=rows),
        lead=2)


def _mixers_kernel(*refs, branches, G):
    n_in = sum(len(b.in_specs) for b in branches)
    n_out = sum(len(b.out_specs) for b in branches)
    pools = [list(refs[:n_in]), list(refs[n_in:n_in + n_out]), list(refs[n_in + n_out:])]
    iters, lead = [], []
    for b in branches:
        marked = [b.in_specs, b.out_specs, b.scratch]
        mine = [[pool.pop(0) for _ in group] for pool, group in zip(pools, marked)]
        for g in range(G):
            view = [r.at[g] if isinstance(m, _Seq) else r
                    for group, ms in zip(mine, marked) for r, m in zip(group, ms)]
            iters.append(b.stages(*view))
            lead.append(b.lead)
    _run_staged(iters, lead)


def _mixers(branches, grid, G):
    outs = pl.pallas_call(
        functools.partial(_mixers_kernel, branches=branches, G=G),
        grid=grid,
        in_specs=_unmark([s for b in branches for s in b.in_specs]),
        out_specs=_unmark([s for b in branches for s in b.out_specs]),
        out_shape=[s for b in branches for s in b.out_shapes],
        scratch_shapes=_unmark([s for b in branches for s in b.scratch]),
        compiler_params=_cparams(len(grid)),
        name="mixers",
    )(*[a for b in branches for a in b.args])
    split, k = [], 0
    for b in branches:
        split.append(outs[k:k + len(b.out_specs)])
        k += len(b.out_specs)
    return split


def _merge_kernel(ya_ref, yb_ref, yc_ref, yd_ref, g_ref, x_ref, wb_ref, wo_ref, fg_ref, o_ref, mg_ref,
                  *, D, W, final):
    ys = (ya_ref, yb_ref, yc_ref, yd_ref)
    for c in range(0, D, W):
        acc = None
        for i in range(N_BRANCH):
            p = _dot(ys[i][...], wb_ref[i, :, c:c + W])
            t = _sigmoid(g_ref[:, i * D + c:i * D + c + W].astype(F32)) * p
            acc = t if acc is None else acc + t
        mg_ref[:, c:c + W] = acc.astype(BF16)
    o = x_ref[...] + _dot(mg_ref[...], wo_ref[...])
    if final:
        ms = jnp.mean(o * o, axis=-1, keepdims=True)
        o = (o * lax.rsqrt(ms + NORM_EPS)) * fg_ref[...]
    o_ref[...] = o


def _merge(ys, pg2, x2, wb, wo, final_g, tm, final):
    N, D = x2.shape
    W = D // N_BRANCH
    yblk = pl.BlockSpec((tm, W), lambda i: (i, 0))
    kern = functools.partial(_merge_kernel, D=D, W=W, final=final)
    return pl.pallas_call(
        kern,
        grid=(N // tm,),
        in_specs=[yblk, yblk, yblk, yblk,
                  pl.BlockSpec((tm, N_BRANCH * D), lambda i: (i, 1)),
                  pl.BlockSpec((tm, D), lambda i: (i, 0)),
                  pl.BlockSpec((N_BRANCH, W, D), lambda i: (0, 0, 0)),
                  pl.BlockSpec((D, D), lambda i: (0, 0)),
                  pl.BlockSpec((1, D), lambda i: (0, 0))],
        out_specs=pl.BlockSpec((tm, D), lambda i: (i, 0)),
        out_shape=jax.ShapeDtypeStruct((N, D), F32),
        scratch_shapes=[pltpu.VMEM((tm, D), BF16)],
        compiler_params=_cparams(1),
        name="merge",
    )(*ys, pg2, x2, wb, wo, final_g.reshape(1, D))


def _rope_tables(past_len, T, step=256):
    half = HEAD_DIM // 2
    inv = ROPE_BASE ** (-jnp.arange(half, dtype=F32) / half)
    angles = lambda p: p.astype(F32)[:, None] * inv[None, :]
    if T % step == 0 and T > step:
        a, b = angles(past_len + jnp.arange(0, T, step)), angles(jnp.arange(step))
        ca, sa, cb, sb = jnp.cos(a)[:, None], jnp.sin(a)[:, None], jnp.cos(b)[None], jnp.sin(b)[None]
        cos, sin = (ca * cb - sa * sb).reshape(T, half), (sa * cb + ca * sb).reshape(T, half)
    else:
        ang = angles(past_len + jnp.arange(T))
        cos, sin = jnp.cos(ang), jnp.sin(ang)
    return jnp.concatenate([cos, cos], axis=1), jnp.concatenate([-sin, sin], axis=1)


def _wprep_kernel(a_ref, b_ref, g_ref, o_ref, og_ref, *, n_gate, plain_blocks, plain_rows, rows):
    j = pl.program_id(0)
    tn, D = a_ref.shape

    def copy(first_shifted):
        for r in range(0, tn, rows):
            if r < first_shifted:
                v = a_ref[r:r + rows, :]
            elif r + rows + n_gate <= tn:
                v = a_ref[r + n_gate:r + rows + n_gate, :]
            else:
                v = jnp.concatenate([a_ref[r + n_gate:tn, :], b_ref[...]], axis=0)
            o_ref[r:r + rows, :] = v.astype(BF16)

    @pl.when(j < plain_blocks)
    def _():
        copy(tn)

    @pl.when(j == plain_blocks)
    def _():
        copy(plain_rows)

    @pl.when(j > plain_blocks)
    def _():
        copy(0)

    @pl.when(j == 0)
    def _():
        og_ref[...] = jnp.concatenate([g_ref[...], jnp.zeros((LANES - n_gate, D), F32)], axis=0).astype(BF16)


def _wprep(wt_in, layer, H):
    _, n_in, D = wt_in.shape
    W = D // N_BRANCH
    n_gate = 2 * H
    assert n_gate == SUBLANES
    s = (ZB + 1) * W
    NC = n_in - n_gate
    tn = 2 * W
    rows = 256
    assert (s % tn) % rows == 0
    kern = functools.partial(_wprep_kernel, n_gate=n_gate, plain_blocks=s // tn, plain_rows=s % tn, rows=rows)
    return pl.pallas_call(
        kern,
        grid=(NC // tn,),
        in_specs=[pl.BlockSpec((None, tn, D), lambda j: (layer, j, 0)),
                  pl.BlockSpec((None, n_gate, D), lambda j: (layer, (j + 1) * (tn // n_gate), 0)),
                  pl.BlockSpec((None, n_gate, D), lambda j: (layer, s // n_gate, 0))],
        out_specs=[pl.BlockSpec((tn, D), lambda j: (j, 0)),
                   pl.BlockSpec((LANES, D), lambda j: (0, 0))],
        out_shape=[jax.ShapeDtypeStruct((NC, D), BF16), jax.ShapeDtypeStruct((LANES, D), BF16)],
        compiler_params=_cparams(1),
        name="wprep",
    )(wt_in, wt_in, wt_in)


def _prep_layer(wt_in, layer, b_i_l, b_f_l, w_branch_l, w_out_l, H):
    w_main, w_if = _wprep(wt_in, layer, H)
    bias_if = jnp.pad(jnp.concatenate([b_i_l, b_f_l]).astype(F32), (0, LANES - 2 * H)).reshape(1, LANES)
    return w_main, w_if, bias_if, w_branch_l.astype(BF16), w_out_l.astype(BF16)


def _trunk(x, past_len, cache_k, cache_v, C0, n0, m0, S0, conv0, layers, conv_w, conv_b, ln_g, ln_b,
           norm_g, final_g, cfg):
    B, T, D = x.shape
    N = B * T
    W = D // N_BRANCH
    H = W // HEAD_DIM
    depth = len(layers)
    cos2, sin2 = _rope_tables(past_len, T)
    x2 = x.reshape(N, D)
    if cache_k is None:
        ck4 = cv4 = None
    else:
        ck4 = cache_k.reshape(depth, B, -1, HEAD_DIM)
        cv4 = cache_v.reshape(depth, B, -1, HEAD_DIM)
    states = []
    kv = None
    for l in range(depth):
        w_main, w_if, bias_if, wb, wo = layers[l]
        pg, k_all, v_all, ifp = _inproj(x2, norm_g[l], w_main, w_if, kv, l, depth, H, cfg["tm_in"])
        kv = (k_all, v_all)
        pg3 = pg.reshape(B, T, -1)
        ya = _stick_breaking(pg3, ck4, cv4, l, H, cfg["tq"], cfg["tk"])
        m0b = jnp.broadcast_to(m0[l].astype(F32)[..., None], (B, H, LANES))
        L, G = cfg["L"], cfg["G"]
        (yb, C1, n1, m1), (yc, S1), (yd, conv1) = _mixers(
            [_mlstm_branch(pg3, ifp.reshape(B, T, LANES), bias_if, C0, n0, m0b, l, L, G),
             _retention_branch(pg3, cos2, sin2, S0, l, L, G),
             _conv_branch(pg3, conv0[l], conv_w[l], conv_b[l], ln_g[l], ln_b[l], L, G)],
            grid=(B // G, T // L), G=G)
        ys = tuple(y.reshape(N, W) for y in (ya, yb, yc, yd))
        x2 = _merge(ys, pg, x2, wb, wo, final_g, cfg["tm_mg"], final=(l == depth - 1))
        states.append((C1, n1, m1[..., 0], S1, conv1))
    stacked = tuple(jnp.stack(s, axis=0) for s in zip(*states))
    kv5 = tuple(a.reshape(depth, B, T, H, HEAD_DIM) for a in kv)
    return x2.reshape(B, T, D), kv5 + stacked


def _config(B, T):
    if T >= 256:
        return dict(tm_in=1024, tq=256, tk=256, L=256, tm_mg=256, G=1)
    return dict(tm_in=512, tq=T, tk=128, L=T, tm_mg=256, G=4 if B % 4 == 0 else 1)


def kernel(x_prompt, x_sample, cache_sb_k, cache_sb_v, state_mlstm_C, state_mlstm_n, state_mlstm_m,
           state_ret_S, state_conv, norm_g, w_in, mlstm_b_i, mlstm_b_f, conv_w, conv_b, conv_ln_g,
           conv_ln_b, w_branch, w_out, final_g):
    depth = w_in.shape[0]
    Bp, Tp, D = x_prompt.shape
    W = D // N_BRANCH
    H = W // HEAD_DIM
    wt_in = jnp.swapaxes(w_in, 1, 2)
    layers = [_prep_layer(wt_in, l, mlstm_b_i[l], mlstm_b_f[l], w_branch[l], w_out[l], H) for l in range(depth)]
    zC = jnp.zeros((depth, Bp, H, HEAD_DIM, HEAD_DIM), F32)
    zn = jnp.zeros((depth, Bp, H, HEAD_DIM), F32)
    zm = jnp.zeros((depth, Bp, H), F32)
    zconv = jnp.zeros((depth, Bp, CONV_WIDTH - 1, W), x_prompt.dtype)
    shared = (layers, conv_w, conv_b, conv_ln_g, conv_ln_b, norm_g, final_g)
    y_p, (pk, pv, pC, pn, pm, pS, pconv) = _trunk(
        x_prompt, 0, None, None, zC, zn, zm, zC, zconv, *shared, _config(Bp, Tp))
    y_s, (sk, sv, sC, sn, sm, sS, sconv) = _trunk(
        x_sample, cache_sb_k.shape[2], cache_sb_k, cache_sb_v, state_mlstm_C, state_mlstm_n,
        state_mlstm_m, state_ret_S, state_conv, *shared, _config(*x_sample.shape[:2]))
    return (y_p, y_s, pk, pv, pC, pn, pm, pS, pconv, sk, sv, sC, sn, sm, sS, sconv)
```
